```python
import math
import jax, jax.numpy as jnp
from jax import lax
import numpy as np

D_MODEL = 1024
BATCH = 2
SEQ = 8192
DEPTH = 4
DEC_BATCH = 128
DEC_SEQ = 1
PAST_LEN = 2048
PAGE_SIZE = 128

H_A = D_MODEL // 128
DK_A = 64
DV_A = 2 * DK_A
QKA_W = H_A * 2 * DK_A
VA_W = H_A * DV_A
H_R = D_MODEL // 128
DK_R = 64
DV_R = 128
QKR_W = H_R * DK_R
VR_W = H_R * DV_R
D_FF = 4 * D_MODEL
Q_BLOCK = 128
RET_CHUNK = 128
EPS = 1e-6
NEG_INF = -1e30

kernel_name = 'hybrid_diffattn_retnet_decode_step'


def rmsnorm(x, g):
    xf = x.astype(jnp.float32)
    y = xf * lax.rsqrt(jnp.mean(xf * xf, axis=-1, keepdims=True) + EPS)
    return (y * g.astype(jnp.float32)).astype(x.dtype)


def adaln(c, w, b):
    m = jnp.einsum('bd,de->be', jax.nn.silu(c), w) + b
    return jnp.split(m[:, None, :], 6, axis=-1)


def modulate(x, g, shift, scale):
    return rmsnorm(x, g) * (1.0 + scale) + shift


def project_heads(h, w_in, q_g, k_g):
    B, S, _ = h.shape
    sizes = (QKA_W, QKA_W, VA_W, QKR_W, QKR_W, VR_W, VR_W, D_MODEL, D_MODEL)
    points = [sum(sizes[:i]) for i in range(1, len(sizes))]
    z = jnp.einsum('bsd,de->bse', h, w_in)
    qa, ka, va, qr, kr, vr, gr, ga, gb = jnp.split(z, points, axis=-1)
    qa = rmsnorm(qa.reshape(B, S, H_A, 2, DK_A), q_g)
    ka = rmsnorm(ka.reshape(B, S, H_A, 2, DK_A), k_g)
    va = va.reshape(B, S, H_A, DV_A)
    qr = qr.reshape(B, S, H_R, DK_R).transpose(0, 2, 1, 3)
    kr = (kr * (DK_R ** -0.5)).reshape(B, S, H_R, DK_R).transpose(0, 2, 1, 3)
    vr = vr.reshape(B, S, H_R, DV_R).transpose(0, 2, 1, 3)
    return qa, ka, va, qr, kr, vr, gr, ga, gb


def diff_logits(q, k, q_pos, k_pos, slopes):
    s = jnp.einsum('bqhcd,bkhcd->bhcqk', q, k, preferred_element_type=jnp.float32) * (DK_A ** -0.5)
    dist = (q_pos[:, None] - k_pos[None, :]).astype(jnp.float32)
    s = s - slopes[None, :, None, None, None] * dist
    return jnp.where(dist >= 0, s, NEG_INF)


def diff_weights(logits, lam):
    p = jax.nn.softmax(logits, axis=-1)
    return p[:, :, 0] - lam * p[:, :, 1]


def diff_attn_prompt(q, k, v, lam, slopes):
    B, S = q.shape[:2]
    nb = S // Q_BLOCK
    qb = q.reshape(B, nb, Q_BLOCK, H_A, 2, DK_A).transpose(1, 0, 2, 3, 4, 5)
    k_pos = jnp.arange(S)

    def block(args):
        qi, i = args
        q_pos = i * Q_BLOCK + jnp.arange(Q_BLOCK)
        w = diff_weights(diff_logits(qi, k, q_pos, k_pos, slopes), lam)
        return jnp.einsum('bhqk,bkhd->bqhd', w, v)

    o = lax.map(block, (qb, jnp.arange(nb)))
    return o.transpose(1, 0, 2, 3, 4).reshape(B, S, H_A, DV_A)


def diff_attn_sample(q, k_new, v_new, k_past, v_past, lam, slopes):
    P = k_past.shape[1]
    Sq = q.shape[1]
    q_pos = P + jnp.arange(Sq)
    lp = diff_logits(q, k_past, q_pos, jnp.arange(P), slopes)
    ln = diff_logits(q, k_new, q_pos, q_pos, slopes)
    w = diff_weights(jnp.concatenate([lp, ln], axis=-1), lam)
    return (jnp.einsum('bhqk,bkhd->bqhd', w[..., :P], v_past)
            + jnp.einsum('bhqk,bkhd->bqhd', w[..., P:], v_new))


def retention_chunk(s0, q, k, v, log_g):
    L = q.shape[2]
    idx = jnp.arange(L, dtype=jnp.float32)
    diff = idx[:, None] - idx[None, :]
    decay = jnp.where(diff >= 0, jnp.exp(log_g[:, None, None] * jnp.maximum(diff, 0.0)), 0.0)
    inner = jnp.einsum('bhid,bhjd->bhij', q, k, preferred_element_type=jnp.float32) * decay
    cross = jnp.exp(log_g[:, None] * (idx[None, :] + 1.0))
    o = (jnp.einsum('bhij,bhjv->bhiv', inner, v)
         + jnp.einsum('bhid,bhdv->bhiv', q, s0) * cross[None, :, :, None])
    k_dec = k * jnp.exp(log_g[:, None] * (L - 1.0 - idx[None, :]))[None, :, :, None]
    s1 = s0 * jnp.exp(log_g * L)[None, :, None, None] + jnp.einsum('bhjd,bhjv->bhdv', k_dec, v)
    return s1, o


def retention_prompt(q, k, v, log_g):
    B, H, S, _ = q.shape
    n_chunks = S // RET_CHUNK

    def to_chunks(t):
        return t.reshape(B, H, n_chunks, RET_CHUNK, t.shape[-1]).transpose(2, 0, 1, 3, 4)

    s0 = jnp.zeros((B, H, DK_R, DV_R), jnp.float32)
    s_fin, oc = lax.scan(lambda s, xs: retention_chunk(s, xs[0], xs[1], xs[2], log_g),
                         s0, (to_chunks(q), to_chunks(k), to_chunks(v)))
    return oc.transpose(1, 2, 0, 3, 4).reshape(B, H, S, DV_R), s_fin


def merge_branches(oa, orr, gr, ga, gb, subln_g, ret_g, lam_init, w_out):
    B, S = oa.shape[:2]
    dt = gr.dtype
    ya = (rmsnorm(oa.astype(dt), subln_g) * (1.0 - lam_init)).reshape(B, S, VA_W)
    yr = rmsnorm(orr.transpose(0, 2, 1, 3).astype(dt), ret_g).reshape(B, S, VR_W) * jax.nn.silu(gr)
    mixed = jax.nn.sigmoid(ga) * ya + jax.nn.sigmoid(gb) * yr
    return jnp.einsum('bse,ed->bsd', mixed, w_out)


def sq_relu_mlp(h, w_up, w_down):
    u = jax.nn.relu(jnp.einsum('bsd,df->bsf', h, w_up))
    return jnp.einsum('bsf,fd->bsd', u * u, w_down)


def setup_inputs(seed: int = 0) -> dict:
    key = jax.random.key(seed)
    ks = jax.random.split(key, 24)
    n_pages = PAST_LEN // PAGE_SIZE
    n_used = DEC_BATCH * n_pages
    n_phys = n_used + (n_used + 3) // 4
    in_w = 2 * QKA_W + VA_W + 2 * QKR_W + 2 * VR_W + 2 * D_MODEL
    f32 = jnp.float32
    nrm = lambda k, shp, s: jax.random.normal(k, shp, f32) * s
    perm = jax.random.permutation(ks[5], n_phys)
    return {
        'x_prompt': nrm(ks[0], (BATCH, SEQ, D_MODEL), 1.0),
        'x_sample': nrm(ks[1], (DEC_BATCH, DEC_SEQ, D_MODEL), 1.0),
        'cache_k': nrm(ks[2], (DEPTH, n_phys, PAGE_SIZE, H_A, 2, DK_A), 1.0),
        'cache_v': nrm(ks[3], (DEPTH, n_phys, PAGE_SIZE, H_A, DV_A), 1.0),
        'state_ret': nrm(ks[4], (DEPTH, DEC_BATCH, H_R, DK_R, DV_R), 0.1),
        'page_table': perm[:n_used].reshape(DEC_BATCH, n_pages).astype(jnp.int32),
        'c_prompt': nrm(ks[6], (BATCH, D_MODEL), 1.0),
        'c_sample': nrm(ks[7], (DEC_BATCH, D_MODEL), 1.0),
        'ada_w': nrm(ks[8], (DEPTH, D_MODEL, 6 * D_MODEL), 0.5 * D_MODEL ** -0.5),
        'ada_b': nrm(ks[9], (DEPTH, 6 * D_MODEL), 0.02),
        'norm1_g': 1.0 + nrm(ks[10], (DEPTH, D_MODEL), 0.02),
        'norm2_g': 1.0 + nrm(ks[11], (DEPTH, D_MODEL), 0.02),
        'w_in': nrm(ks[12], (DEPTH, D_MODEL, in_w), D_MODEL ** -0.5),
        'q_norm_g': 1.0 + nrm(ks[13], (DEPTH, DK_A), 0.02),
        'k_norm_g': 1.0 + nrm(ks[14], (DEPTH, DK_A), 0.02),
        'lam_q1': nrm(ks[15], (DEPTH, DK_A), 0.1),
        'lam_k1': nrm(ks[16], (DEPTH, DK_A), 0.1),
        'lam_q2': nrm(ks[17], (DEPTH, DK_A), 0.1),
        'lam_k2': nrm(ks[18], (DEPTH, DK_A), 0.1),
        'subln_g': 1.0 + nrm(ks[19], (DEPTH, DV_A), 0.02),
        'ret_norm_g': 1.0 + nrm(ks[20], (DEPTH, DV_R), 0.02),
        'w_out': nrm(ks[21], (DEPTH, VA_W, D_MODEL), VA_W ** -0.5),
        'w_up': nrm(ks[22], (DEPTH, D_MODEL, D_FF), D_MODEL ** -0.5),
        'w_down': nrm(ks[23], (DEPTH, D_FF, D_MODEL), D_FF ** -0.5),
    }


def reference(x_prompt, x_sample, cache_k, cache_v, state_ret, page_table, c_prompt, c_sample,
              ada_w, ada_b, norm1_g, norm2_g, w_in, q_norm_g, k_norm_g,
              lam_q1, lam_k1, lam_q2, lam_k2, subln_g, ret_norm_g, w_out, w_up, w_down):
    f32 = jnp.float32
    slopes = jnp.exp2(-8.0 * jnp.arange(1, H_A + 1, dtype=f32) / H_A)
    log_g = jnp.log1p(-jnp.exp2(-5.0 - jnp.arange(H_R, dtype=f32)))
    db, n_pages = page_table.shape
    past = n_pages * cache_k.shape[2]
    xp, xs = x_prompt, x_sample
    kp_l, vp_l, sp_l, ks_l, vs_l, ss_l = [], [], [], [], [], []
    for l in range(DEPTH):
        lam_init = 0.8 - 0.6 * math.exp(-0.3 * l)
        lam = (jnp.exp(jnp.sum(lam_q1[l].astype(f32) * lam_k1[l].astype(f32)))
               - jnp.exp(jnp.sum(lam_q2[l].astype(f32) * lam_k2[l].astype(f32))) + lam_init)

        sh_a, sc_a, gt_a, sh_m, sc_m, gt_m = adaln(c_prompt, ada_w[l], ada_b[l])
        h = modulate(xp, norm1_g[l], sh_a, sc_a)
        qa, ka, va, qr, kr, vr, gr, ga, gb = project_heads(h, w_in[l], q_norm_g[l], k_norm_g[l])
        oa = diff_attn_prompt(qa, ka, va, lam, slopes)
        orr, s_fin = retention_prompt(qr, kr, vr, log_g)
        xp = xp + gt_a * merge_branches(oa, orr, gr, ga, gb, subln_g[l], ret_norm_g[l], lam_init, w_out[l])
        xp = xp + gt_m * sq_relu_mlp(modulate(xp, norm2_g[l], sh_m, sc_m), w_up[l], w_down[l])
        kp_l.append(ka)
        vp_l.append(va)
        sp_l.append(s_fin.astype(state_ret.dtype))

        sh_a, sc_a, gt_a, sh_m, sc_m, gt_m = adaln(c_sample, ada_w[l], ada_b[l])
        h = modulate(xs, norm1_g[l], sh_a, sc_a)
        qa, ka, va, qr, kr, vr, gr, ga, gb = project_heads(h, w_in[l], q_norm_g[l], k_norm_g[l])
        k_past = cache_k[l, page_table].reshape(db, past, H_A, 2, DK_A)
        v_past = cache_v[l, page_table].reshape(db, past, H_A, DV_A)
        oa = diff_attn_sample(qa, ka, va, k_past, v_past, lam, slopes)
        s_new, orr = retention_chunk(state_ret[l].astype(f32), qr, kr, vr, log_g)
        xs = xs + gt_a * merge_branches(oa, orr, gr, ga, gb, subln_g[l], ret_norm_g[l], lam_init, w_out[l])
        xs = xs + gt_m * sq_relu_mlp(modulate(xs, norm2_g[l], sh_m, sc_m), w_up[l], w_down[l])
        ks_l.append(ka)
        vs_l.append(va)
        ss_l.append(s_new.astype(state_ret.dtype))

    return (xp, xs, jnp.stack(kp_l), jnp.stack(vp_l), jnp.stack(sp_l),
            jnp.stack(ks_l), jnp.stack(vs_l), jnp.stack(ss_l))
```

```python
import functools
import math

import numpy as np
import jax
import jax.numpy as jnp
from jax import lax
from jax.experimental import pallas as pl
from jax.experimental.pallas import tpu as pltpu

F32 = jnp.float32
BF16 = jnp.bfloat16

D_MODEL = 1024
N_HEADS = 8
HEAD_W = 128
SUBLANES = 8
DK = 64
N_GROUPS = 8
EPS = 1e-6
NEG_INF = -1e30
LOG2E = 1.4426950408889634
QK_SCALE = DK ** -0.5

VMEM_LIMIT = 56 * 1024 * 1024


def _cparams(*sem):
    return pltpu.CompilerParams(dimension_semantics=sem, vmem_limit_bytes=VMEM_LIMIT)


def _silu(x):
    return x * jax.nn.sigmoid(x)


def _dot(a, b):
    return jnp.dot(a, b, preferred_element_type=F32)


def _dot_nt(a, b):
    return lax.dot_general(a, b, (((1,), (1,)), ((), ())), preferred_element_type=F32)


def _dot_tn(a, b):
    return lax.dot_general(a, b, (((0,), (0,)), ((), ())), preferred_element_type=F32)


def _adaln_kernel(cp_ref, cs_ref, w_ref, b_ref, op_ref, os_ref):
    w = w_ref[0].astype(BF16)
    b = b_ref[0, 0]
    op_ref[0, 0] = _dot(_silu(cp_ref[...]).astype(BF16), w) + b
    os_ref[0, 0] = _dot(_silu(cs_ref[...]).astype(BF16), w) + b


def _adaln(c_p, c_s, ada_w, ada_b):
    depth = ada_w.shape[0]
    bp, bs = c_p.shape[0], c_s.shape[0]
    d = D_MODEL
    b4 = ada_b.reshape(depth, 6, 1, d)
    return pl.pallas_call(
        _adaln_kernel,
        grid=(depth, 6),
        in_specs=[
            pl.BlockSpec((bp, d), lambda l, j: (0, 0)),
            pl.BlockSpec((bs, d), lambda l, j: (0, 0)),
            pl.BlockSpec((1, d, d), lambda l, j: (l, 0, j)),
            pl.BlockSpec((1, 1, 1, d), lambda l, j: (l, j, 0, 0)),
        ],
        out_specs=[
            pl.BlockSpec((1, 1, bp, d), lambda l, j: (l, j, 0, 0)),
            pl.BlockSpec((1, 1, bs, d), lambda l, j: (l, j, 0, 0)),
        ],
        out_shape=[
            jax.ShapeDtypeStruct((depth, 6, bp, d), F32),
            jax.ShapeDtypeStruct((depth, 6, bs, d), F32),
        ],
        compiler_params=_cparams("arbitrary", "arbitrary"),
        name="adaln",
    )(c_p, c_s, ada_w, b4)


def _modulated_norm(x, g, shift, scale):
    ms = jnp.mean(x * x, axis=-1, keepdims=True)
    return (x * lax.rsqrt(ms + EPS) * g) * (1.0 + scale) + shift


def _proj_kernel(x_ref, mod_ref, g1_ref, w_ref, qg_ref, kg_ref,
                 q_ref, kf_ref, kb_ref, vf_ref, vb_ref, qr_ref, kr_ref, vr_ref,
                 gr_ref, ga_ref, gb_ref, *, per_row_mod):
    if per_row_mod:
        shift, scale = mod_ref[0], mod_ref[1]
    else:
        shift, scale = mod_ref[0, 0], mod_ref[1, 0]
    h = _modulated_norm(x_ref[...], g1_ref[...], shift, scale).astype(BF16)

    def group(c):
        return _dot(h, w_ref[:, c * D_MODEL:(c + 1) * D_MODEL])

    lo = lax.broadcasted_iota(jnp.int32, (1, HEAD_W), 1) < DK

    def qk_norm(z, g, emit):
        for hd in range(N_HEADS):
            zc = z[:, hd * HEAD_W:(hd + 1) * HEAD_W]
            sq = zc * zc
            s_lo = jnp.sum(jnp.where(lo, sq, 0.0), axis=-1, keepdims=True)
            s_all = jnp.sum(sq, axis=-1, keepdims=True)
            ms = jnp.where(lo, s_lo, s_all - s_lo) * (1.0 / DK)
            emit(hd, zc * lax.rsqrt(ms + EPS) * g)

    def emit_q(hd, y):
        q_ref[:, hd * HEAD_W:(hd + 1) * HEAD_W] = (y * (QK_SCALE * LOG2E)).astype(BF16)

    def emit_k(hd, y):
        kf_ref[:, hd * HEAD_W:(hd + 1) * HEAD_W] = y
        kb_ref[:, hd * HEAD_W:(hd + 1) * HEAD_W] = y.astype(BF16)

    qk_norm(group(0), qg_ref[...], emit_q)
    qk_norm(group(1), kg_ref[...], emit_k)
    va = group(2)
    vf_ref[...] = va
    vb_ref[...] = va.astype(BF16)
    z = group(3)
    half = D_MODEL // 2
    qr_ref[...] = z[:, :half].astype(BF16)
    kr_ref[...] = (z[:, half:] * QK_SCALE).astype(BF16)
    vr_ref[...] = group(4).astype(BF16)
    gr_ref[...] = group(5)
    ga_ref[...] = group(6)
    gb_ref[...] = group(7)


def _proj(x, mods, g1, w_in, qg, kg, *, tm, rows_per_batch):
    t, d = x.shape
    per_row = rows_per_batch is None
    if per_row:
        mod_spec = pl.BlockSpec((6, tm, d), lambda i: (0, i, 0))
    else:
        tiles_per_batch = rows_per_batch // tm
        mod_spec = pl.BlockSpec((6, 1, 1, d), lambda i: (0, i // tiles_per_batch, 0, 0))
    row = lambda w: pl.BlockSpec((tm, w), lambda i: (i, 0))
    const = lambda shp: pl.BlockSpec(shp, lambda i: (0,) * len(shp))
    widths = (d, d, d, d, d, d // 2, d // 2, d, d, d, d)
    dtypes = (BF16, F32, BF16, F32, BF16, BF16, BF16, BF16, F32, F32, F32)
    return pl.pallas_call(
        functools.partial(_proj_kernel, per_row_mod=per_row),
        grid=(t // tm,),
        in_specs=[
            row(d), mod_spec, const((1, d)),
            pl.BlockSpec(w_in.shape, lambda i: (0, 0), pipeline_mode=pl.Buffered(1)),
            const((1, HEAD_W)), const((1, HEAD_W)),
        ],
        out_specs=[row(w) for w in widths],
        out_shape=[jax.ShapeDtypeStruct((t, w), dt) for w, dt in zip(widths, dtypes)],
        compiler_params=_cparams("arbitrary"),
        name="proj",
    )(x, mods, g1, w_in, qg, kg)


def _lambda(lq1, lk1, lq2, lk2, lam_init):
    a = jnp.sum(lq1 * lk1, axis=-1, keepdims=True)
    b = jnp.sum(lq2 * lk2, axis=-1, keepdims=True)
    return jnp.exp(a) - jnp.exp(b) + lam_init


def _attn_kernel(slope_ref, q_ref, k_ref, v_ref, lq1_ref, lk1_ref, lq2_ref, lk2_ref, o_ref,
                 m_scr, l_scr, acc_scr, *, tq, lam_init):
    hd = pl.program_id(1)
    i = pl.program_id(2)
    tk = tq
    c2 = slope_ref[hd]
    q = q_ref[0]
    lane = lax.broadcasted_iota(jnp.int32, q.shape, 1)
    zero = jnp.zeros_like(q)
    qq = jnp.concatenate([jnp.where(lane < DK, q, zero), jnp.where(lane >= DK, q, zero)], axis=0)

    m_scr[...] = jnp.full(m_scr.shape, NEG_INF, F32)
    l_scr[...] = jnp.zeros(l_scr.shape, F32)
    acc_scr[...] = jnp.zeros(acc_scr.shape, F32)
    col = lax.broadcasted_iota(jnp.int32, (1, tk), 1)

    def step(j, masked):
        start = pl.multiple_of(j * tk, tk)
        k = k_ref[0, pl.ds(start, tk), :]
        v = v_ref[0, pl.ds(start, tk), :]
        s = _dot_nt(qq, k)
        s = s + (col + (j - i) * tk).astype(F32) * c2
        if masked:
            r = lax.broadcasted_iota(jnp.int32, s.shape, 0)
            cc = lax.broadcasted_iota(jnp.int32, s.shape, 1)
            s = jnp.where(cc <= jnp.where(r >= tq, r - tq, r), s, NEG_INF)
        m_prev = m_scr[...]
        m_new = jnp.maximum(m_prev, jnp.max(s, axis=-1, keepdims=True))
        alpha = jnp.exp2(m_prev - m_new)
        p = jnp.exp2(s - pltpu.repeat(m_new, tk // HEAD_W, axis=1))
        l_scr[...] = alpha * l_scr[...] + jnp.sum(p, axis=-1, keepdims=True)
        acc_scr[...] = alpha * acc_scr[...] + _dot(p.astype(BF16), v)
        m_scr[...] = m_new

    def body(j, carry):
        step(j, False)
        return carry

    lax.fori_loop(0, i, body, 0)
    step(i, True)

    o = acc_scr[...] / l_scr[...]
    lam = _lambda(lq1_ref[...], lk1_ref[...], lq2_ref[...], lk2_ref[...], lam_init)
    o_ref[0] = o[:tq] - lam * o[tq:]


def _attn_prompt(q, k, v, lam_vecs, lam_init, *, tq):
    b, s, d = q.shape
    slopes = jnp.asarray(LOG2E * np.exp2(-np.arange(1, N_HEADS + 1, dtype=np.float64)), F32)
    vec = pl.BlockSpec((1, DK), lambda bb, h, i: (0, 0))
    kv = pl.BlockSpec((1, s, HEAD_W), lambda bb, h, i: (bb, 0, h))
    return pl.pallas_call(
        functools.partial(_attn_kernel, tq=tq, lam_init=lam_init),
        grid=(b, N_HEADS, s // tq),
        in_specs=[
            pl.BlockSpec(memory_space=pltpu.SMEM),
            pl.BlockSpec((1, tq, HEAD_W), lambda bb, h, i: (bb, i, h)),
            kv, kv, vec, vec, vec, vec,
        ],
        out_specs=pl.BlockSpec((1, tq, HEAD_W), lambda bb, h, i: (bb, i, h)),
        out_shape=jax.ShapeDtypeStruct((b, s, d), F32),
        scratch_shapes=[pltpu.VMEM((2 * tq, HEAD_W), F32)] * 3,
        compiler_params=_cparams("arbitrary", "arbitrary", "arbitrary"),
        name="attn_prompt",
    )(slopes, q, k, v, *lam_vecs)


def _ret_tables(chunk):
    log_g = np.log1p(-np.exp2(-5.0 - np.arange(N_HEADS, dtype=np.float64)))
    idx = np.arange(chunk, dtype=np.float64)
    diff = idx[:, None] - idx[None, :]
    dec = np.where(diff >= 0, np.exp(log_g[:, None, None] * np.maximum(diff, 0.0)), 0.0)
    cross = np.exp(log_g[:, None] * (idx[None, :] + 1.0))
    kdec = np.exp(log_g[:, None] * (chunk - 1.0 - idx[None, :]))
    rep = lambda a: np.repeat(a[:, :, None], HEAD_W, axis=2)
    return (jnp.asarray(dec, F32), jnp.asarray(rep(cross), F32), jnp.asarray(rep(kdec), F32),
            jnp.asarray(np.exp(log_g * chunk), F32))


def _ret_kernel(gl_ref, q_ref, k_ref, v_ref, dec_ref, cross_ref, kdec_ref, o_ref, fin_ref, st_scr):
    pair = pl.program_id(1)
    c = pl.program_id(2)

    @pl.when(c == 0)
    def _():
        st_scr[...] = jnp.zeros(st_scr.shape, F32)

    qb = q_ref[0]
    kb = k_ref[0]
    lane = lax.broadcasted_iota(jnp.int32, qb.shape, 1)
    zero = jnp.zeros_like(qb)
    for hh in range(2):
        msk = (lane < DK) if hh == 0 else (lane >= DK)
        qh = jnp.where(msk, qb, zero)
        kh = jnp.where(msk, kb, zero)
        vh = v_ref[0, :, hh * HEAD_W:(hh + 1) * HEAD_W]
        st = st_scr[hh]
        inner = _dot_nt(qh, kh) * dec_ref[hh]
        o = _dot(inner.astype(BF16), vh) + _dot(qh, st.astype(BF16)) * cross_ref[hh]
        o_ref[0, :, hh * HEAD_W:(hh + 1) * HEAD_W] = o
        kd = (kh.astype(F32) * kdec_ref[hh]).astype(BF16)
        st_scr[hh] = st * gl_ref[2 * pair + hh] + _dot_tn(kd, vh)

    @pl.when(c == pl.num_programs(2) - 1)
    def _():
        fin_ref[0, 0] = st_scr[0, :DK, :]
        fin_ref[0, 1] = st_scr[1, DK:, :]


def _ret_prompt(qr, kr, vr, *, chunk):
    b, s, d = vr.shape
    dec, cross, kdec, gl = _ret_tables(chunk)
    qk = pl.BlockSpec((1, chunk, HEAD_W), lambda bb, p, c: (bb, c, p))
    vo = pl.BlockSpec((1, chunk, 2 * HEAD_W), lambda bb, p, c: (bb, c, p))
    tab = lambda w: pl.BlockSpec((2, chunk, w), lambda bb, p, c: (p, 0, 0))
    return pl.pallas_call(
        _ret_kernel,
        grid=(b, N_HEADS // 2, s // chunk),
        in_specs=[pl.BlockSpec(memory_space=pltpu.SMEM), qk, qk, vo,
                  tab(chunk), tab(HEAD_W), tab(HEAD_W)],
        out_specs=[vo, pl.BlockSpec((1, 2, DK, HEAD_W), lambda bb, p, c: (bb, p, 0, 0))],
        out_shape=[jax.ShapeDtypeStruct((b, s, d), F32),
                   jax.ShapeDtypeStruct((b, N_HEADS, DK, HEAD_W), F32)],
        scratch_shapes=[pltpu.VMEM((2, HEAD_W, HEAD_W), F32)],
        compiler_params=_cparams("arbitrary", "arbitrary", "arbitrary"),
        name="ret_prompt",
    )(gl, qr, kr, vr, dec, cross, kdec)


def _head_norm(t, g):
    parts = []
    for hd in range(N_HEADS):
        tc = t[:, hd * HEAD_W:(hd + 1) * HEAD_W]
        ms = jnp.mean(tc * tc, axis=-1, keepdims=True)
        parts.append(tc * lax.rsqrt(ms + EPS) * g)
    return jnp.concatenate(parts, axis=-1)


def _merge_mlp_kernel(x_ref, oa_ref, or_ref, gr_ref, ga_ref, gb_ref, mod_ref, sg_ref, rg_ref, n2_ref,
                      wo_ref, wu_ref, wd_ref, y_ref, *, per_row_mod, lam_init):
    if per_row_mod:
        gt_a, sh_m, sc_m, gt_m = (mod_ref[c] for c in (2, 3, 4, 5))
    else:
        gt_a, sh_m, sc_m, gt_m = (mod_ref[c, 0] for c in (2, 3, 4, 5))
    ya = _head_norm(oa_ref[...], sg_ref[...]) * (1.0 - lam_init)
    yr = _head_norm(or_ref[...], rg_ref[...]) * _silu(gr_ref[...])
    mixed = jax.nn.sigmoid(ga_ref[...]) * ya + jax.nn.sigmoid(gb_ref[...]) * yr
    x1 = x_ref[...] + gt_a * _dot(mixed.astype(BF16), wo_ref[...])
    h2 = _modulated_norm(x1, n2_ref[...], sh_m, sc_m).astype(BF16)
    u = jnp.maximum(_dot(h2, wu_ref[...]), 0.0)
    y_ref[...] = x1 + gt_m * _dot((u * u).astype(BF16), wd_ref[...])


def _merge_mlp(x, oa, orr, gr, ga, gb, mods, sg, rg, n2, w_out, w_up, w_down, lam_init, *, tm,
               rows_per_batch):
    t, d = x.shape
    per_row = rows_per_batch is None
    if per_row:
        mod_spec = pl.BlockSpec((6, tm, d), lambda i: (0, i, 0))
    else:
        tiles_per_batch = rows_per_batch // tm
        mod_spec = pl.BlockSpec((6, 1, 1, d), lambda i: (0, i // tiles_per_batch, 0, 0))
    row = pl.BlockSpec((tm, d), lambda i: (i, 0))
    const = lambda shp: pl.BlockSpec(shp, lambda i: (0,) * len(shp))
    weight = lambda w: pl.BlockSpec(w.shape, lambda i: (0, 0), pipeline_mode=pl.Buffered(1))
    return pl.pallas_call(
        functools.partial(_merge_mlp_kernel, per_row_mod=per_row, lam_init=lam_init),
        grid=(t // tm,),
        in_specs=[row] * 6 + [mod_spec, const((1, HEAD_W)), const((1, HEAD_W)), const((1, d)),
                               weight(w_out), weight(w_up), weight(w_down)],
        out_specs=row,
        out_shape=jax.ShapeDtypeStruct((t, d), F32),
        compiler_params=_cparams("arbitrary"),
        name="merge_mlp",
    )(x, oa, orr, gr, ga, gb, mods, sg, rg, n2, w_out, w_up, w_down)


def _attn_decode_kernel(pt_ref, slope_ref, q_ref, kn_ref, vn_ref, lq1_ref, lk1_ref, lq2_ref, lk2_ref,
                        *rest, pages_per_step, page, past, lam_init):
    k_refs = rest[:pages_per_step]
    v_refs = rest[pages_per_step:2 * pages_per_step]
    o_ref, m_scr, l_scr, acc_scr = rest[2 * pages_per_step:]
    g = pl.program_id(1)
    nrow = 2 * N_HEADS
    d = D_MODEL

    q = q_ref[0].astype(F32)
    row = lax.broadcasted_iota(jnp.int32, (nrow, d), 0)
    lane = lax.broadcasted_iota(jnp.int32, (nrow, d), 1)
    own = (lane // DK) == row
    qbd = jnp.where(own, jnp.broadcast_to(q, (nrow, d)), 0.0)
    c2 = slope_ref[...]

    @pl.when(g == 0)
    def _():
        kn = kn_ref[0].astype(F32)
        m0 = jnp.sum(qbd * kn, axis=-1, keepdims=True)
        m_scr[...] = jnp.broadcast_to(m0, m_scr.shape)
        l_scr[...] = jnp.ones(l_scr.shape, F32)
        acc_scr[...] = jnp.broadcast_to(vn_ref[0].astype(F32), acc_scr.shape)

    qb = qbd.astype(BF16)
    pos = lax.broadcasted_iota(jnp.int32, (1, page), 1)
    s_parts = []
    for t in range(pages_per_step):
        kp = k_refs[t][0, 0].astype(BF16)
        s = _dot_nt(qb, kp)
        kpos = (g * pages_per_step + t) * page + pos
        s_parts.append(s + (kpos - past).astype(F32) * c2)
    m_prev = m_scr[...]
    m_new = m_prev
    for s in s_parts:
        m_new = jnp.maximum(m_new, jnp.max(s, axis=-1, keepdims=True))
    alpha = jnp.exp2(m_prev - m_new)
    l_new = alpha * l_scr[...]
    acc = alpha[:, :1] * acc_scr[...]
    for t, s in enumerate(s_parts):
        p = jnp.exp2(s - m_new)
        l_new = l_new + jnp.sum(p, axis=-1, keepdims=True)
        acc = acc + _dot(p.astype(BF16), v_refs[t][0, 0].astype(BF16))
    m_scr[...] = m_new
    l_scr[...] = l_new
    acc_scr[...] = acc

    @pl.when(g == pl.num_programs(1) - 1)
    def _():
        lam = _lambda(lq1_ref[...], lk1_ref[...], lq2_ref[...], lk2_ref[...], lam_init)
        inv = 1.0 / l_new[:, :1]
        head = lane // HEAD_W
        coef = jnp.where(row == 2 * head, inv, jnp.where(row == 2 * head + 1, -lam * inv, 0.0))
        o_ref[0] = jnp.sum(acc * coef, axis=0, keepdims=True)


def _attn_decode(q, k_new, v_new, cache_k, cache_v, layer, page_table, lam_vecs, lam_init, *,
                 pages_per_step):
    n, _, d = q.shape
    n_pages = page_table.shape[1]
    page = cache_k.shape[2]
    steps = n_pages // pages_per_step
    slopes = np.repeat(LOG2E * np.exp2(-np.arange(1, N_HEADS + 1, dtype=np.float64)), 2)
    slopes = jnp.asarray(np.repeat(slopes[:, None], HEAD_W, axis=1), F32)
    tok = pl.BlockSpec((1, 1, d), lambda b, g, pt: (b, 0, 0))
    vec = pl.BlockSpec((1, DK), lambda b, g, pt: (0, 0))

    def page_spec(t):
        return pl.BlockSpec(
            (1, 1, page, d),
            lambda b, g, pt: (layer, pt[b * n_pages + g * pages_per_step + t], 0, 0))

    kernel = functools.partial(_attn_decode_kernel, pages_per_step=pages_per_step, page=page,
                               past=n_pages * page, lam_init=lam_init)
    return pl.pallas_call(
        kernel,
        grid_spec=pltpu.PrefetchScalarGridSpec(
            num_scalar_prefetch=1,
            grid=(n, steps),
            in_specs=[pl.BlockSpec((2 * N_HEADS, HEAD_W), lambda b, g, pt: (0, 0)),
                      tok, tok, tok, vec, vec, vec, vec]
                     + [page_spec(t) for t in range(pages_per_step)] * 2,
            out_specs=tok,
            scratch_shapes=[pltpu.VMEM((2 * N_HEADS, HEAD_W), F32),
                            pltpu.VMEM((2 * N_HEADS, HEAD_W), F32),
                            pltpu.VMEM((2 * N_HEADS, d), F32)],
        ),
        out_shape=jax.ShapeDtypeStruct((n, 1, d), F32),
        compiler_params=_cparams("arbitrary", "arbitrary"),
        name="attn_decode",
    )(page_table.reshape(-1), slopes, q, k_new, v_new, *lam_vecs,
      *([cache_k] * pages_per_step), *([cache_v] * pages_per_step))


def _ret_step_kernel(q_ref, k_ref, v_ref, gam_ref, s_ref, o_ref, sn_ref, *, bb):
    rows = N_HEADS * DK
    eye = (lax.broadcasted_iota(jnp.int32, (rows, rows), 0)
           == lax.broadcasted_iota(jnp.int32, (rows, rows), 1))
    gam = gam_ref[...]

    def to_col(r):
        return jnp.sum(jnp.where(eye, jnp.broadcast_to(r, (rows, rows)), 0.0), axis=-1, keepdims=True)

    def head_sum(a):
        return jnp.sum(a.reshape(N_HEADS, DK, a.shape[-1]), axis=1)

    for i in range(bb):
        qcol = to_col(q_ref[i:i + 1, :])
        kcol = to_col(k_ref[i:i + 1, :])
        v8 = v_ref[i]
        vexp = jnp.concatenate(
            [jnp.broadcast_to(v8[hd:hd + 1, :], (DK, HEAD_W)) for hd in range(N_HEADS)], axis=0)
        s0 = s_ref[0, i].reshape(rows, HEAD_W)
        qk = head_sum(jnp.broadcast_to(qcol * kcol, (rows, HEAD_W)))
        o_ref[i] = head_sum(s0 * gam * qcol) + qk * v8
        sn_ref[i] = (s0 * gam + kcol * vexp).reshape(N_HEADS, DK, HEAD_W)


def _ret_step(qr, kr, vr, state, layer, *, bb):
    n = qr.shape[0]
    gamma = 1.0 - np.exp2(-5.0 - np.arange(N_HEADS, dtype=np.float64))
    gam = jnp.asarray(np.repeat(np.repeat(gamma, DK)[:, None], HEAD_W, axis=1), F32)
    qk = pl.BlockSpec((bb, N_HEADS * DK), lambda i: (i, 0))
    hv = pl.BlockSpec((bb, N_HEADS, HEAD_W), lambda i: (i, 0, 0))
    return pl.pallas_call(
        functools.partial(_ret_step_kernel, bb=bb),
        grid=(n // bb,),
        in_specs=[qk, qk, hv, pl.BlockSpec(gam.shape, lambda i: (0, 0)),
                  pl.BlockSpec((1, bb, N_HEADS, DK, HEAD_W), lambda i: (layer, i, 0, 0, 0))],
        out_specs=[hv, pl.BlockSpec((bb, N_HEADS, DK, HEAD_W), lambda i: (i, 0, 0, 0))],
        out_shape=[jax.ShapeDtypeStruct((n, N_HEADS, HEAD_W), F32),
                   jax.ShapeDtypeStruct((n, N_HEADS, DK, HEAD_W), F32)],
        compiler_params=_cparams("arbitrary"),
        name="ret_step",
    )(qr, kr, vr, gam, state)


def kernel(x_prompt, x_sample, cache_k, cache_v, state_ret, page_table, c_prompt, c_sample, ada_w, ada_b, norm1_g, norm2_g, w_in, q_norm_g, k_norm_g, lam_q1, lam_k1, lam_q2, lam_k2, subln_g, ret_norm_g, w_out, w_up, w_down):
    depth = ada_w.shape[0]
    bp, s, d = x_prompt.shape
    ns = x_sample.shape[0]
    n_phys, page = cache_k.shape[1], cache_k.shape[2]
    ck = cache_k.reshape(depth, n_phys, page, d)
    cv = cache_v.reshape(depth, n_phys, page, d)

    bp_pad = -(-bp // SUBLANES) * SUBLANES
    c_p = jnp.pad(c_prompt, ((0, bp_pad - bp), (0, 0)))
    mods_p, mods_s = _adaln(c_p, c_sample, ada_w, ada_b)
    mods_p = mods_p.reshape(depth, 6, bp_pad, 1, d)
    w_in_b, w_out_b = w_in.astype(BF16), w_out.astype(BF16)
    w_up_b, w_down_b = w_up.astype(BF16), w_down.astype(BF16)

    xp = x_prompt.reshape(bp * s, d)
    xs = x_sample.reshape(ns, d)
    kp_l, vp_l, sp_l, ks_l, vs_l, ss_l = [], [], [], [], [], []
    for l in range(depth):
        lam_init = 0.8 - 0.6 * math.exp(-0.3 * l)
        lam_vecs = [a[l].reshape(1, DK) for a in (lam_q1, lam_k1, lam_q2, lam_k2)]
        g1, n2 = norm1_g[l].reshape(1, d), norm2_g[l].reshape(1, d)
        qg = jnp.tile(q_norm_g[l], 2).reshape(1, HEAD_W)
        kg = jnp.tile(k_norm_g[l], 2).reshape(1, HEAD_W)
        sg, rg = subln_g[l].reshape(1, HEAD_W), ret_norm_g[l].reshape(1, HEAD_W)

        q, kf, kb, vf, vb, qr, kr, vr, gr, ga, gb = _proj(
            xp, mods_p[l], g1, w_in_b[l], qg, kg, tm=256, rows_per_batch=s)
        b3 = lambda a: a.reshape(bp, s, a.shape[-1])
        oa = _attn_prompt(b3(q), b3(kb), b3(vb), lam_vecs, lam_init, tq=512)
        orr, s_fin = _ret_prompt(b3(qr), b3(kr), b3(vr), chunk=256)
        xp = _merge_mlp(xp, oa.reshape(bp * s, d), orr.reshape(bp * s, d), gr, ga, gb, mods_p[l],
                        sg, rg, n2, w_out_b[l], w_up_b[l], w_down_b[l], lam_init,
                        tm=256, rows_per_batch=s)
        kp_l.append(kf)
        vp_l.append(vf)
        sp_l.append(s_fin)

        q, kf, kb, vf, vb, qr, kr, vr, gr, ga, gb = _proj(
            xs, mods_s[l], g1, w_in_b[l], qg, kg, tm=ns, rows_per_batch=None)
        t3 = lambda a: a.reshape(ns, 1, d)
        oa = _attn_decode(t3(q), t3(kb), t3(vb), ck, cv, l, page_table, lam_vecs, lam_init,
                          pages_per_step=8)
        orr, s_new = _ret_step(qr.astype(F32), kr.astype(F32),
                               vr.astype(F32).reshape(ns, N_HEADS, HEAD_W), state_ret, l, bb=8)
        xs = _merge_mlp(xs, oa.reshape(ns, d), orr.reshape(ns, d), gr, ga, gb, mods_s[l],
                        sg, rg, n2, w_out_b[l], w_up_b[l], w_down_b[l], lam_init,
                        tm=ns, rows_per_batch=None)
        ks_l.append(kf)
        vs_l.append(vf)
        ss_l.append(s_new)

    h2 = (N_HEADS, 2, DK)
    return (xp.reshape(bp, s, d), xs.reshape(ns, 1, d),
            jnp.stack(kp_l).reshape(depth, bp, s, *h2),
            jnp.stack(vp_l).reshape(depth, bp, s, N_HEADS, HEAD_W),
            jnp.stack(sp_l),
            jnp.stack(ks_l).reshape(depth, ns, 1, *h2),
            jnp.stack(vs_l).reshape(depth, ns, 1, N_HEADS, HEAD_W),
            jnp.stack(ss_l))
```

```python
import functools
import math

import numpy as np
import jax
import jax.numpy as jnp
from jax import lax
from jax.experimental import pallas as pl
from jax.experimental.pallas import tpu as pltpu

F32 = jnp.float32
BF16 = jnp.bfloat16

D_MODEL = 1024
N_HEADS = 8
HEAD_W = 128
SUBLANES = 8
DK = 64
N_GROUPS = 8
EPS = 1e-6
NEG_INF = -1e30
LOG2E = 1.4426950408889634
QK_SCALE = DK ** -0.5

VMEM_LIMIT = 56 * 1024 * 1024


def _cparams(*sem):
    return pltpu.CompilerParams(dimension_semantics=sem, vmem_limit_bytes=VMEM_LIMIT)


def _silu(x):
    return x * jax.nn.sigmoid(x)


def _dot(a, b):
    return jnp.dot(a, b, preferred_element_type=F32)


def _dot_nt(a, b):
    return lax.dot_general(a, b, (((1,), (1,)), ((), ())), preferred_element_type=F32)


def _dot_tn(a, b):
    return lax.dot_general(a, b, (((0,), (0,)), ((), ())), preferred_element_type=F32)


def _adaln_kernel(cp_ref, cs_ref, w_ref, b_ref, op_ref, os_ref):
    w = w_ref[0].astype(BF16)
    b = b_ref[0, 0]
    op_ref[0, 0] = _dot(_silu(cp_ref[...]).astype(BF16), w) + b
    os_ref[0, 0] = _dot(_silu(cs_ref[...]).astype(BF16), w) + b


def _adaln(c_p, c_s, ada_w, ada_b):
    depth = ada_w.shape[0]
    bp, bs = c_p.shape[0], c_s.shape[0]
    d = D_MODEL
    b4 = ada_b.reshape(depth, 6, 1, d)
    return pl.pallas_call(
        _adaln_kernel,
        grid=(depth, 6),
        in_specs=[
            pl.BlockSpec((bp, d), lambda l, j: (0, 0)),
            pl.BlockSpec((bs, d), lambda l, j: (0, 0)),
            pl.BlockSpec((1, d, d), lambda l, j: (l, 0, j)),
            pl.BlockSpec((1, 1, 1, d), lambda l, j: (l, j, 0, 0)),
        ],
        out_specs=[
            pl.BlockSpec((1, 1, bp, d), lambda l, j: (l, j, 0, 0)),
            pl.BlockSpec((1, 1, bs, d), lambda l, j: (l, j, 0, 0)),
        ],
        out_shape=[
            jax.ShapeDtypeStruct((depth, 6, bp, d), F32),
            jax.ShapeDtypeStruct((depth, 6, bs, d), F32),
        ],
        compiler_params=_cparams("arbitrary", "arbitrary"),
        name="adaln",
    )(c_p, c_s, ada_w, b4)


def _modulated_norm(x, g, shift, scale):
    ms = jnp.mean(x * x, axis=-1, keepdims=True)
    return (x * lax.rsqrt(ms + EPS) * g) * (1.0 + scale) + shift


def _proj_kernel(x_ref, mod_ref, g1_ref, w_ref, qg_ref, kg_ref, *rest, per_row_mod, n_aliased):
    (q_ref, ktf_ref, ktb_ref, kb_ref, vf_ref, vb_ref, qr_ref, kr_ref, vr_ref,
     gr_ref, ga_ref, gb_ref) = rest[n_aliased:]
    if per_row_mod:
        shift, scale = mod_ref[0], mod_ref[1]
    else:
        shift, scale = mod_ref[0, 0], mod_ref[1, 0]
    h = _modulated_norm(x_ref[...], g1_ref[...], shift, scale).astype(BF16)

    def group(c):
        return _dot(h, w_ref[:, c * D_MODEL:(c + 1) * D_MODEL])

    lo = lax.broadcasted_iota(jnp.int32, (1, HEAD_W), 1) < DK

    def qk_norm(z, g, emit):
        for hd in range(N_HEADS):
            zc = z[:, hd * HEAD_W:(hd + 1) * HEAD_W]
            sq = zc * zc
            s_lo = jnp.sum(jnp.where(lo, sq, 0.0), axis=-1, keepdims=True)
            s_all = jnp.sum(sq, axis=-1, keepdims=True)
            ms = jnp.where(lo, s_lo, s_all - s_lo) * (1.0 / DK)
            emit(hd, zc * lax.rsqrt(ms + EPS) * g)

    def emit_q(hd, y):
        q_ref[:, hd * HEAD_W:(hd + 1) * HEAD_W] = (y * (QK_SCALE * LOG2E)).astype(BF16)

    def emit_k(hd, y):
        yt = y.T
        ktf_ref[hd * HEAD_W:(hd + 1) * HEAD_W, :] = yt
        ktb_ref[hd * HEAD_W:(hd + 1) * HEAD_W, :] = yt.astype(BF16)
        kb_ref[:, hd * HEAD_W:(hd + 1) * HEAD_W] = y.astype(BF16)

    qk_norm(group(0), qg_ref[...], emit_q)
    qk_norm(group(1), kg_ref[...], emit_k)
    va = group(2)
    vf_ref[...] = va
    vb_ref[...] = va.astype(BF16)
    z = group(3)
    half = D_MODEL // 2
    qr_ref[...] = z[:, :half].astype(BF16)
    kr_ref[...] = (z[:, half:] * QK_SCALE).astype(BF16)
    vr_ref[...] = group(4).astype(BF16)
    gr_ref[...] = group(5)
    ga_ref[...] = group(6)
    gb_ref[...] = group(7)


def _proj(x, mods, g1, w_in, qg, kg, *, tm, rows_per_batch=None, stacked=None, layer=None):
    t, d = x.shape
    row = lambda w: pl.BlockSpec((tm, w), lambda i: (i, 0))
    const = lambda shp: pl.BlockSpec(shp, lambda i: (0,) * len(shp))
    sds = jax.ShapeDtypeStruct
    if stacked is None:
        mod_spec = pl.BlockSpec((6, tm, d), lambda i: (0, i, 0))
        kt_spec = pl.BlockSpec((d, tm), lambda i: (0, i))
        ktf = (kt_spec, sds((d, t), F32))
        ktb = (kt_spec, sds((d, t), BF16))
        vf = (row(d), sds((t, d), F32))
        extra_in, extra_specs, aliases = [], [], {}
    else:
        k_all, v_all = stacked
        s = rows_per_batch
        tpb = s // tm
        mod_spec = pl.BlockSpec((6, 1, 1, d), lambda i: (0, i // tpb, 0, 0))
        ktf = (pl.BlockSpec((None, None, d, tm), lambda i: (layer, i // tpb, 0, i % tpb)),
               sds(k_all.shape, F32))
        ktb = (pl.BlockSpec((None, d, tm), lambda i: (i // tpb, 0, i % tpb)), sds((t // s, d, s), BF16))
        vf = (pl.BlockSpec((None, tm, d), lambda i: (layer, i, 0)), sds(v_all.shape, F32))
        extra_in = [k_all, v_all]
        extra_specs = [pl.BlockSpec(memory_space=pl.ANY)] * 2
        aliases = {6: 1, 7: 4}
    plain = lambda w, dt: (row(w), sds((t, w), dt))
    outs = [plain(d, BF16), ktf, ktb, plain(d, BF16), vf, plain(d, BF16),
            plain(d // 2, BF16), plain(d // 2, BF16), plain(d, BF16),
            plain(d, F32), plain(d, F32), plain(d, F32)]
    return pl.pallas_call(
        functools.partial(_proj_kernel, per_row_mod=stacked is None, n_aliased=len(extra_in)),
        grid=(t // tm,),
        in_specs=[
            row(d), mod_spec, const((1, d)),
            pl.BlockSpec(w_in.shape, lambda i: (0, 0), pipeline_mode=pl.Buffered(1)),
            const((1, HEAD_W)), const((1, HEAD_W)),
        ] + extra_specs,
        out_specs=[o[0] for o in outs],
        out_shape=[o[1] for o in outs],
        input_output_aliases=aliases,
        compiler_params=_cparams("arbitrary"),
        name="proj",
    )(x, mods, g1, w_in, qg, kg, *extra_in)


def _lambda(lq1, lk1, lq2, lk2, lam_init):
    a = jnp.sum(lq1 * lk1, axis=-1, keepdims=True)
    b = jnp.sum(lq2 * lk2, axis=-1, keepdims=True)
    return jnp.exp(a) - jnp.exp(b) + lam_init


BIAS_ROWS = 16
BIAS_TERMS = 3


def _attn_kernel(slope_ref, q_ref, kt_ref, v_ref, lq1_ref, lk1_ref, lq2_ref, lk2_ref, o_ref,
                 kaug_scr, vaug_scr, qq_scr, sa_scr, sb_scr, m_scr, l_scr, acc_scr, *, tq, lam_init):
    hd = pl.program_id(1)
    i = pl.program_id(2)
    tk = tq
    s_len = kt_ref.shape[-1]

    @pl.when(i == 0)
    def _():
        kaug_scr[:HEAD_W, :] = kt_ref[0]
        r = lax.broadcasted_iota(jnp.int32, (BIAS_ROWS, s_len), 0)
        x = lax.broadcasted_iota(jnp.int32, (BIAS_ROWS, s_len), 1).astype(F32) * slope_ref[hd]
        hi = x.astype(BF16).astype(F32)
        mid = (x - hi).astype(BF16).astype(F32)
        lo = x - hi - mid
        pieces = jnp.where(r == 0, hi, jnp.where(r == 1, mid, jnp.where(r == 2, lo, 0.0)))
        kaug_scr[HEAD_W:HEAD_W + BIAS_ROWS, :] = pieces.astype(BF16)
        kaug_scr[HEAD_W + BIAS_ROWS:, :] = jnp.zeros((HEAD_W - BIAS_ROWS, s_len), BF16)
        vaug_scr[:, :HEAD_W] = v_ref[0]
        vaug_scr[:, HEAD_W:] = jnp.ones((s_len, HEAD_W), BF16)

    q = q_ref[0]
    lane = lax.broadcasted_iota(jnp.int32, q.shape, 1)
    zero = jnp.zeros_like(q)
    ones_cols = jnp.where(lane < BIAS_TERMS, 1.0, 0.0).astype(BF16)
    qq_scr[:tq, :HEAD_W] = jnp.where(lane < DK, q, zero)
    qq_scr[tq:, :HEAD_W] = jnp.where(lane >= DK, q, zero)
    qq_scr[:tq, HEAD_W:] = ones_cols
    qq_scr[tq:, HEAD_W:] = ones_cols

    m_scr[...] = jnp.full(m_scr.shape, NEG_INF, F32)
    l_scr[...] = jnp.zeros(l_scr.shape, F32)
    acc_scr[...] = jnp.zeros(acc_scr.shape, F32)

    def logits(j, s_ref):
        start = pl.multiple_of(j * tk, tk)
        s_ref[...] = _dot(qq_scr[...], kaug_scr[:, pl.ds(start, tk)])

    def absorb(j, s_ref, masked):
        start = pl.multiple_of(j * tk, tk)
        s = s_ref[...]
        if masked:
            r = lax.broadcasted_iota(jnp.int32, s.shape, 0)
            cc = lax.broadcasted_iota(jnp.int32, s.shape, 1)
            s = jnp.where(cc <= jnp.where(r >= tq, r - tq, r), s, NEG_INF)
        m_prev = m_scr[...]
        m_new = jnp.maximum(m_prev, jnp.max(s, axis=-1, keepdims=True))
        alpha = jnp.exp2(m_prev - m_new)
        p = jnp.exp2(s - pltpu.repeat(m_new, tk // HEAD_W, axis=1))
        pv = _dot(p.astype(BF16), vaug_scr[pl.ds(start, tk), :])
        l_scr[...] = alpha * l_scr[...] + pv[:, HEAD_W:]
        acc_scr[...] = alpha * acc_scr[...] + pv[:, :HEAD_W]
        m_scr[...] = m_new

    logits(0, sa_scr)

    def pair(t, carry):
        j = 2 * t
        logits(j + 1, sb_scr)
        absorb(j, sa_scr, False)
        logits(j + 2, sa_scr)
        absorb(j + 1, sb_scr, False)
        return carry

    lax.fori_loop(0, i // 2, pair, 0)

    @pl.when(i % 2 == 1)
    def _():
        logits(i, sb_scr)
        absorb(i - 1, sa_scr, False)
        absorb(i, sb_scr, True)

    @pl.when(i % 2 == 0)
    def _():
        absorb(i, sa_scr, True)

    o = acc_scr[...] / l_scr[...]
    lam = _lambda(lq1_ref[...], lk1_ref[...], lq2_ref[...], lk2_ref[...], lam_init)
    o_ref[0] = o[:tq] - lam * o[tq:]


def _attn_prompt(q, kt, v, lam_vecs, lam_init, *, tq):
    b, s, d = q.shape
    slopes = jnp.asarray(LOG2E * np.exp2(-np.arange(1, N_HEADS + 1, dtype=np.float64)), F32)
    vec = pl.BlockSpec((1, DK), lambda bb, h, i: (0, 0))
    return pl.pallas_call(
        functools.partial(_attn_kernel, tq=tq, lam_init=lam_init),
        grid=(b, N_HEADS, s // tq),
        in_specs=[
            pl.BlockSpec(memory_space=pltpu.SMEM),
            pl.BlockSpec((1, tq, HEAD_W), lambda bb, h, i: (bb, i, h)),
            pl.BlockSpec((1, HEAD_W, s), lambda bb, h, i: (bb, h, 0)),
            pl.BlockSpec((1, s, HEAD_W), lambda bb, h, i: (bb, 0, h)),
            vec, vec, vec, vec,
        ],
        out_specs=pl.BlockSpec((1, tq, HEAD_W), lambda bb, h, i: (bb, i, h)),
        out_shape=jax.ShapeDtypeStruct((b, s, d), F32),
        scratch_shapes=[pltpu.VMEM((2 * HEAD_W, s), BF16), pltpu.VMEM((s, 2 * HEAD_W), BF16),
                        pltpu.VMEM((2 * tq, 2 * HEAD_W), BF16)]
                       + [pltpu.VMEM((2 * tq, tq), F32)] * 2
                       + [pltpu.VMEM((2 * tq, HEAD_W), F32)] * 3,
        compiler_params=_cparams("arbitrary", "arbitrary", "arbitrary"),
        name="attn_prompt",
    )(slopes, q, kt, v, *lam_vecs)


def _ret_tables(chunk):
    log_g = np.log1p(-np.exp2(-5.0 - np.arange(N_HEADS, dtype=np.float64)))
    idx = np.arange(chunk, dtype=np.float64)
    diff = idx[:, None] - idx[None, :]
    dec = np.where(diff >= 0, np.exp(log_g[:, None, None] * np.maximum(diff, 0.0)), 0.0)
    cross = np.exp(log_g[:, None] * (idx[None, :] + 1.0))
    kdec = np.exp(log_g[:, None] * (chunk - 1.0 - idx[None, :]))
    rep = lambda a: np.repeat(a[:, :, None], HEAD_W, axis=2)
    return (jnp.asarray(dec, F32), jnp.asarray(rep(cross), F32), jnp.asarray(rep(kdec), F32),
            jnp.asarray(np.exp(log_g * chunk), F32))


def _ret_kernel(gl_ref, q_ref, k_ref, v_ref, dec_ref, cross_ref, kdec_ref, o_ref, fin_ref, st_scr):
    pair = pl.program_id(1)
    c = pl.program_id(2)

    @pl.when(c == 0)
    def _():
        st_scr[...] = jnp.zeros(st_scr.shape, F32)

    qb = q_ref[0]
    kb = k_ref[0]
    lane = lax.broadcasted_iota(jnp.int32, qb.shape, 1)
    zero = jnp.zeros_like(qb)
    for hh in range(2):
        msk = (lane < DK) if hh == 0 else (lane >= DK)
        qh = jnp.where(msk, qb, zero)
        kh = jnp.where(msk, kb, zero)
        vh = v_ref[0, :, hh * HEAD_W:(hh + 1) * HEAD_W]
        st = st_scr[hh]
        inner = _dot_nt(qh, kh) * dec_ref[hh]
        o = _dot(inner.astype(BF16), vh) + _dot(qh, st.astype(BF16)) * cross_ref[hh]
        o_ref[0, :, hh * HEAD_W:(hh + 1) * HEAD_W] = o
        kd = (kh.astype(F32) * kdec_ref[hh]).astype(BF16)
        st_scr[hh] = st * gl_ref[2 * pair + hh] + _dot_tn(kd, vh)

    @pl.when(c == pl.num_programs(2) - 1)
    def _():
        fin_ref[0, 0] = st_scr[0, :DK, :]
        fin_ref[0, 1] = st_scr[1, DK:, :]


def _ret_prompt(qr, kr, vr, *, chunk):
    b, s, d = vr.shape
    dec, cross, kdec, gl = _ret_tables(chunk)
    qk = pl.BlockSpec((1, chunk, HEAD_W), lambda bb, p, c: (bb, c, p))
    vo = pl.BlockSpec((1, chunk, 2 * HEAD_W), lambda bb, p, c: (bb, c, p))
    tab = lambda w: pl.BlockSpec((2, chunk, w), lambda bb, p, c: (p, 0, 0))
    return pl.pallas_call(
        _ret_kernel,
        grid=(b, N_HEADS // 2, s // chunk),
        in_specs=[pl.BlockSpec(memory_space=pltpu.SMEM), qk, qk, vo,
                  tab(chunk), tab(HEAD_W), tab(HEAD_W)],
        out_specs=[vo, pl.BlockSpec((1, 2, DK, HEAD_W), lambda bb, p, c: (bb, p, 0, 0))],
        out_shape=[jax.ShapeDtypeStruct((b, s, d), F32),
                   jax.ShapeDtypeStruct((b, N_HEADS, DK, HEAD_W), F32)],
        scratch_shapes=[pltpu.VMEM((2, HEAD_W, HEAD_W), F32)],
        compiler_params=_cparams("arbitrary", "arbitrary", "arbitrary"),
        name="ret_prompt",
    )(gl, qr, kr, vr, dec, cross, kdec)


def _head_norm(t, g):
    parts = []
    for hd in range(N_HEADS):
        tc = t[:, hd * HEAD_W:(hd + 1) * HEAD_W]
        ms = jnp.mean(tc * tc, axis=-1, keepdims=True)
        parts.append(tc * lax.rsqrt(ms + EPS) * g)
    return jnp.concatenate(parts, axis=-1)


def _merge_mlp_kernel(x_ref, oa_ref, or_ref, gr_ref, ga_ref, gb_ref, mod_ref, sg_ref, rg_ref, n2_ref,
                      wo_ref, wu_ref, wd_ref, y_ref, *, per_row_mod, lam_init):
    if per_row_mod:
        gt_a, sh_m, sc_m, gt_m = (mod_ref[c] for c in (2, 3, 4, 5))
    else:
        gt_a, sh_m, sc_m, gt_m = (mod_ref[c, 0] for c in (2, 3, 4, 5))
    ya = _head_norm(oa_ref[...], sg_ref[...]) * (1.0 - lam_init)
    yr = _head_norm(or_ref[...], rg_ref[...]) * _silu(gr_ref[...])
    mixed = jax.nn.sigmoid(ga_ref[...]) * ya + jax.nn.sigmoid(gb_ref[...]) * yr
    x1 = x_ref[...] + gt_a * _dot(mixed.astype(BF16), wo_ref[...])
    h2 = _modulated_norm(x1, n2_ref[...], sh_m, sc_m).astype(BF16)
    u = jnp.maximum(_dot(h2, wu_ref[...]), 0.0)
    y_ref[...] = x1 + gt_m * _dot((u * u).astype(BF16), wd_ref[...])


def _merge_mlp(x, oa, orr, gr, ga, gb, mods, sg, rg, n2, w_out, w_up, w_down, lam_init, *, tm,
               rows_per_batch):
    t, d = x.shape
    per_row = rows_per_batch is None
    if per_row:
        mod_spec = pl.BlockSpec((6, tm, d), lambda i: (0, i, 0))
    else:
        tiles_per_batch = rows_per_batch // tm
        mod_spec = pl.BlockSpec((6, 1, 1, d), lambda i: (0, i // tiles_per_batch, 0, 0))
    row = pl.BlockSpec((tm, d), lambda i: (i, 0))
    const = lambda shp: pl.BlockSpec(shp, lambda i: (0,) * len(shp))
    weight = lambda w: pl.BlockSpec(w.shape, lambda i: (0, 0), pipeline_mode=pl.Buffered(1))
    return pl.pallas_call(
        functools.partial(_merge_mlp_kernel, per_row_mod=per_row, lam_init=lam_init),
        grid=(t // tm,),
        in_specs=[row] * 6 + [mod_spec, const((1, HEAD_W)), const((1, HEAD_W)), const((1, d)),
                               weight(w_out), weight(w_up), weight(w_down)],
        out_specs=row,
        out_shape=jax.ShapeDtypeStruct((t, d), F32),
        compiler_params=_cparams("arbitrary"),
        name="merge_mlp",
    )(x, oa, orr, gr, ga, gb, mods, sg, rg, n2, w_out, w_up, w_down)


def _attn_decode_kernel(pt_ref, slope_ref, expand_ref, q_ref, kn_ref, vn_ref,
                        lq1_ref, lk1_ref, lq2_ref, lk2_ref, *rest, pages_per_step, page, past, lam_init):
    k_refs = rest[:pages_per_step]
    v_refs = rest[pages_per_step:2 * pages_per_step]
    o_ref, m_scr, l_scr, acc_scr = rest[2 * pages_per_step:]
    g = pl.program_id(1)
    nrow = 2 * N_HEADS
    d = D_MODEL

    q = q_ref[0].astype(F32)
    row = lax.broadcasted_iota(jnp.int32, (nrow, d), 0)
    lane = lax.broadcasted_iota(jnp.int32, (nrow, d), 1)
    own = (lane // DK) == 2 * (row % N_HEADS) + row // N_HEADS
    qbd = jnp.where(own, jnp.broadcast_to(q, (nrow, d)), 0.0)
    c2 = slope_ref[...]

    @pl.when(g == 0)
    def _():
        kn = kn_ref[0].astype(F32)
        m0 = jnp.sum(qbd * kn, axis=-1, keepdims=True)
        m_scr[...] = jnp.broadcast_to(m0, m_scr.shape)
        l_scr[...] = jnp.ones(l_scr.shape, F32)
        v8 = vn_ref[0].astype(F32)
        acc_scr[...] = jnp.concatenate([v8, v8], axis=0)

    qb = qbd.astype(BF16)
    pos = lax.broadcasted_iota(jnp.int32, (1, page), 1)
    s_parts = []
    for t in range(pages_per_step):
        kt = k_refs[t][0, 0].reshape(d, page).astype(BF16)
        kpos = (g * pages_per_step + t) * page + pos
        s_parts.append(_dot(qb, kt) + (kpos - past).astype(F32) * c2)
    m_prev = m_scr[...]
    m_new = m_prev
    for s in s_parts:
        m_new = jnp.maximum(m_new, jnp.max(s, axis=-1, keepdims=True))
    alpha = jnp.exp2(m_prev - m_new)
    l_new = alpha * l_scr[...]
    acc = alpha * acc_scr[...]
    own_head = (lane % N_HEADS) == (row % N_HEADS)
    for t, s in enumerate(s_parts):
        p = jnp.exp2(s - m_new)
        l_new = l_new + jnp.sum(p, axis=-1, keepdims=True)
        spread = _dot(p.astype(BF16), expand_ref[...])
        pe = jnp.where(own_head, spread, 0.0).astype(BF16)
        acc = acc + _dot(pe, v_refs[t][0, 0].reshape(page * N_HEADS, HEAD_W).astype(BF16))
    m_scr[...] = m_new
    l_scr[...] = l_new
    acc_scr[...] = acc

    @pl.when(g == pl.num_programs(1) - 1)
    def _():
        lam = _lambda(lq1_ref[...], lk1_ref[...], lq2_ref[...], lk2_ref[...], lam_init)
        o = acc / l_new
        o_ref[0] = o[:N_HEADS] - lam * o[N_HEADS:]


def _attn_decode(q, k_new, v_new, cache_kt, cache_v, layer, page_table, lam_vecs, lam_init, *,
                 pages_per_step):
    n, _, d = q.shape
    n_pages = page_table.shape[1]
    page = cache_v.shape[2]
    steps = n_pages // pages_per_step
    slopes = np.tile(LOG2E * np.exp2(-np.arange(1, N_HEADS + 1, dtype=np.float64)), 2)
    slopes = jnp.asarray(np.repeat(slopes[:, None], HEAD_W, axis=1), F32)
    expand = jnp.asarray(np.repeat(np.eye(page), N_HEADS, axis=1), BF16)
    tok = pl.BlockSpec((1, 1, d), lambda b, g, pt: (b, 0, 0))
    hv = pl.BlockSpec((1, N_HEADS, HEAD_W), lambda b, g, pt: (b, 0, 0))
    vec = pl.BlockSpec((1, DK), lambda b, g, pt: (0, 0))
    const = lambda a: pl.BlockSpec(a.shape, lambda b, g, pt: (0, 0))

    def phys(b, g, pt, t):
        return pt[b * n_pages + g * pages_per_step + t]

    def k_spec(t):
        return pl.BlockSpec((1, 1, N_HEADS, 2, DK, page),
                            lambda b, g, pt: (layer, phys(b, g, pt, t), 0, 0, 0, 0))

    def v_spec(t):
        return pl.BlockSpec((1, 1, page, N_HEADS, HEAD_W),
                            lambda b, g, pt: (layer, phys(b, g, pt, t), 0, 0, 0))

    kernel = functools.partial(_attn_decode_kernel, pages_per_step=pages_per_step, page=page,
                               past=n_pages * page, lam_init=lam_init)
    return pl.pallas_call(
        kernel,
        grid_spec=pltpu.PrefetchScalarGridSpec(
            num_scalar_prefetch=1,
            grid=(n, steps),
            in_specs=[const(slopes), const(expand), tok, tok, hv, vec, vec, vec, vec]
                     + [k_spec(t) for t in range(pages_per_step)]
                     + [v_spec(t) for t in range(pages_per_step)],
            out_specs=hv,
            scratch_shapes=[pltpu.VMEM((2 * N_HEADS, HEAD_W), F32)] * 3,
        ),
        out_shape=jax.ShapeDtypeStruct((n, N_HEADS, HEAD_W), F32),
        compiler_params=_cparams("arbitrary", "arbitrary"),
        name="attn_decode",
    )(page_table.reshape(-1), slopes, expand, q, k_new, v_new, *lam_vecs,
      *([cache_kt] * pages_per_step), *([cache_v] * pages_per_step))


def _ret_step_kernel(q_ref, k_ref, v_ref, gam_ref, s_ref, o_ref, sn_ref, *, bb):
    rows = N_HEADS * DK
    eye = (lax.broadcasted_iota(jnp.int32, (rows, rows), 0)
           == lax.broadcasted_iota(jnp.int32, (rows, rows), 1))
    gam = gam_ref[...]

    def to_col(r):
        return jnp.sum(jnp.where(eye, jnp.broadcast_to(r, (rows, rows)), 0.0), axis=-1, keepdims=True)

    def head_sum(a):
        return jnp.sum(a.reshape(N_HEADS, DK, a.shape[-1]), axis=1)

    for i in range(bb):
        qcol = to_col(q_ref[i:i + 1, :])
        kcol = to_col(k_ref[i:i + 1, :])
        v8 = v_ref[i]
        vexp = jnp.concatenate(
            [jnp.broadcast_to(v8[hd:hd + 1, :], (DK, HEAD_W)) for hd in range(N_HEADS)], axis=0)
        s0 = s_ref[0, i].reshape(rows, HEAD_W)
        qk = head_sum(jnp.broadcast_to(qcol * kcol, (rows, HEAD_W)))
        o_ref[i] = head_sum(s0 * gam * qcol) + qk * v8
        sn_ref[i] = (s0 * gam + kcol * vexp).reshape(N_HEADS, DK, HEAD_W)


def _ret_step(qr, kr, vr, state, layer, *, bb):
    n = qr.shape[0]
    gamma = 1.0 - np.exp2(-5.0 - np.arange(N_HEADS, dtype=np.float64))
    gam = jnp.asarray(np.repeat(np.repeat(gamma, DK)[:, None], HEAD_W, axis=1), F32)
    qk = pl.BlockSpec((bb, N_HEADS * DK), lambda i: (i, 0))
    hv = pl.BlockSpec((bb, N_HEADS, HEAD_W), lambda i: (i, 0, 0))
    return pl.pallas_call(
        functools.partial(_ret_step_kernel, bb=bb),
        grid=(n // bb,),
        in_specs=[qk, qk, hv, pl.BlockSpec(gam.shape, lambda i: (0, 0)),
                  pl.BlockSpec((1, bb, N_HEADS, DK, HEAD_W), lambda i: (layer, i, 0, 0, 0))],
        out_specs=[hv, pl.BlockSpec((bb, N_HEADS, DK, HEAD_W), lambda i: (i, 0, 0, 0))],
        out_shape=[jax.ShapeDtypeStruct((n, N_HEADS, HEAD_W), F32),
                   jax.ShapeDtypeStruct((n, N_HEADS, DK, HEAD_W), F32)],
        compiler_params=_cparams("arbitrary"),
        name="ret_step",
    )(qr, kr, vr, gam, state)


def kernel(x_prompt, x_sample, cache_k, cache_v, state_ret, page_table, c_prompt, c_sample, ada_w, ada_b, norm1_g, norm2_g, w_in, q_norm_g, k_norm_g, lam_q1, lam_k1, lam_q2, lam_k2, subln_g, ret_norm_g, w_out, w_up, w_down):
    depth = ada_w.shape[0]
    bp, s, d = x_prompt.shape
    ns = x_sample.shape[0]
    ckt = jnp.transpose(cache_k, (0, 1, 3, 4, 5, 2))

    bp_pad = -(-bp // SUBLANES) * SUBLANES
    c_p = jnp.pad(c_prompt, ((0, bp_pad - bp), (0, 0)))
    mods_p, mods_s = _adaln(c_p, c_sample, ada_w, ada_b)
    mods_p = mods_p.reshape(depth, 6, bp_pad, 1, d)
    w_in_b, w_out_b = w_in.astype(BF16), w_out.astype(BF16)
    w_up_b, w_down_b = w_up.astype(BF16), w_down.astype(BF16)

    xp = x_prompt.reshape(bp * s, d)
    xs = x_sample.reshape(ns, d)
    k_all = jnp.zeros((depth, bp, d, s), F32)
    v_all = jnp.zeros((depth, bp * s, d), F32)
    sp_l, ks_l, vs_l, ss_l = [], [], [], []
    for l in range(depth):
        lam_init = 0.8 - 0.6 * math.exp(-0.3 * l)
        lam_vecs = [a[l].reshape(1, DK) for a in (lam_q1, lam_k1, lam_q2, lam_k2)]
        g1, n2 = norm1_g[l].reshape(1, d), norm2_g[l].reshape(1, d)
        qg = jnp.tile(q_norm_g[l], 2).reshape(1, HEAD_W)
        kg = jnp.tile(k_norm_g[l], 2).reshape(1, HEAD_W)
        sg, rg = subln_g[l].reshape(1, HEAD_W), ret_norm_g[l].reshape(1, HEAD_W)

        q, k_all, ktb, _, v_all, vb, qr, kr, vr, gr, ga, gb = _proj(
            xp, mods_p[l], g1, w_in_b[l], qg, kg, tm=256, rows_per_batch=s,
            stacked=(k_all, v_all), layer=l)
        b3 = lambda a: a.reshape(bp, s, a.shape[-1])
        oa = _attn_prompt(b3(q), ktb, b3(vb), lam_vecs, lam_init, tq=512)
        orr, s_fin = _ret_prompt(b3(qr), b3(kr), b3(vr), chunk=256)
        xp = _merge_mlp(xp, oa.reshape(bp * s, d), orr.reshape(bp * s, d), gr, ga, gb, mods_p[l],
                        sg, rg, n2, w_out_b[l], w_up_b[l], w_down_b[l], lam_init,
                        tm=256, rows_per_batch=s)
        sp_l.append(s_fin)

        q, ktf, _, kb, vf, vb, qr, kr, vr, gr, ga, gb = _proj(
            xs, mods_s[l], g1, w_in_b[l], qg, kg, tm=ns)
        t3 = lambda a: a.reshape(ns, 1, d)
        h3 = lambda a: a.reshape(ns, N_HEADS, HEAD_W)
        oa = _attn_decode(t3(q), t3(kb), h3(vb), ckt, cache_v, l, page_table, lam_vecs, lam_init,
                          pages_per_step=8)
        orr, s_new = _ret_step(qr.astype(F32), kr.astype(F32), h3(vr.astype(F32)), state_ret, l, bb=8)
        xs = _merge_mlp(xs, oa.reshape(ns, d), orr.reshape(ns, d), gr, ga, gb, mods_s[l],
                        sg, rg, n2, w_out_b[l], w_up_b[l], w_down_b[l], lam_init,
                        tm=ns, rows_per_batch=None)
        ks_l.append(ktf)
        vs_l.append(vf)
        ss_l.append(s_new)

    h2 = (N_HEADS, 2, DK)
    k_prompt = jnp.transpose(k_all.reshape(depth, bp, *h2, s), (0, 1, 5, 2, 3, 4))
    k_sample = jnp.transpose(jnp.stack(ks_l).reshape(depth, *h2, ns, 1), (0, 4, 5, 1, 2, 3))
    return (xp.reshape(bp, s, d), xs.reshape(ns, 1, d),
            k_prompt,
            v_all.reshape(depth, bp, s, N_HEADS, HEAD_W),
            jnp.stack(sp_l),
            k_sample,
            jnp.stack(vs_l).reshape(depth, ns, 1, N_HEADS, HEAD_W),
            jnp.stack(ss_l))
```

```python
import functools
import math

import numpy as np
import jax
import jax.numpy as jnp
from jax import lax
from jax.experimental import pallas as pl
from jax.experimental.pallas import tpu as pltpu

F32 = jnp.float32
BF16 = jnp.bfloat16

D_MODEL = 1024
N_HEADS = 8
HEAD_W = 128
SUBLANES = 8
DK = 64
N_GROUPS = 8
EPS = 1e-6
NEG_INF = -1e30
LOG2E = 1.4426950408889634
QK_SCALE = DK ** -0.5

VMEM_LIMIT = 56 * 1024 * 1024


def _cparams(*sem):
    return pltpu.CompilerParams(dimension_semantics=sem, vmem_limit_bytes=VMEM_LIMIT)


def _silu(x):
    return x * jax.nn.sigmoid(x)


def _dot(a, b):
    return jnp.dot(a, b, preferred_element_type=F32)


def _dot_nt(a, b):
    return lax.dot_general(a, b, (((1,), (1,)), ((), ())), preferred_element_type=F32)


def _dot_tn(a, b):
    return lax.dot_general(a, b, (((0,), (0,)), ((), ())), preferred_element_type=F32)


def _adaln_kernel(cp_ref, cs_ref, w_ref, b_ref, op_ref, os_ref):
    w = w_ref[0].astype(BF16)
    b = b_ref[0, 0]
    op_ref[0, 0] = _dot(_silu(cp_ref[...]).astype(BF16), w) + b
    os_ref[0, 0] = _dot(_silu(cs_ref[...]).astype(BF16), w) + b


def _adaln(c_p, c_s, ada_w, ada_b):
    depth = ada_w.shape[0]
    bp, bs = c_p.shape[0], c_s.shape[0]
    d = D_MODEL
    b4 = ada_b.reshape(depth, 6, 1, d)
    return pl.pallas_call(
        _adaln_kernel,
        grid=(depth, 6),
        in_specs=[
            pl.BlockSpec((bp, d), lambda l, j: (0, 0)),
            pl.BlockSpec((bs, d), lambda l, j: (0, 0)),
            pl.BlockSpec((1, d, d), lambda l, j: (l, 0, j)),
            pl.BlockSpec((1, 1, 1, d), lambda l, j: (l, j, 0, 0)),
        ],
        out_specs=[
            pl.BlockSpec((1, 1, bp, d), lambda l, j: (l, j, 0, 0)),
            pl.BlockSpec((1, 1, bs, d), lambda l, j: (l, j, 0, 0)),
        ],
        out_shape=[
            jax.ShapeDtypeStruct((depth, 6, bp, d), F32),
            jax.ShapeDtypeStruct((depth, 6, bs, d), F32),
        ],
        compiler_params=_cparams("arbitrary", "arbitrary"),
        name="adaln",
    )(c_p, c_s, ada_w, b4)


def _modulated_norm(x, g, shift, scale):
    ms = jnp.mean(x * x, axis=-1, keepdims=True)
    return (x * lax.rsqrt(ms + EPS) * g) * (1.0 + scale) + shift


def _proj_kernel(x_ref, mod_ref, g1_ref, w_ref, qg_ref, kg_ref, *rest, per_row_mod, n_aliased):
    (q_ref, ktf_ref, ktb_ref, kb_ref, vf_ref, vb_ref, qr_ref, kr_ref, vr_ref,
     gr_ref, ga_ref, gb_ref) = rest[n_aliased:]
    if per_row_mod:
        shift, scale = mod_ref[0], mod_ref[1]
    else:
        shift, scale = mod_ref[0, 0], mod_ref[1, 0]
    h = _modulated_norm(x_ref[...], g1_ref[...], shift, scale).astype(BF16)

    def group(c):
        return _dot(h, w_ref[:, c * D_MODEL:(c + 1) * D_MODEL])

    lo = lax.broadcasted_iota(jnp.int32, (1, HEAD_W), 1) < DK

    def qk_norm(z, g, emit):
        for hd in range(N_HEADS):
            zc = z[:, hd * HEAD_W:(hd + 1) * HEAD_W]
            sq = zc * zc
            s_lo = jnp.sum(jnp.where(lo, sq, 0.0), axis=-1, keepdims=True)
            s_all = jnp.sum(sq, axis=-1, keepdims=True)
            ms = jnp.where(lo, s_lo, s_all - s_lo) * (1.0 / DK)
            emit(hd, zc * lax.rsqrt(ms + EPS) * g)

    def emit_q(hd, y):
        q_ref[:, hd * HEAD_W:(hd + 1) * HEAD_W] = (y * (QK_SCALE * LOG2E)).astype(BF16)

    def emit_k(hd, y):
        yt = y.T
        ktf_ref[hd * HEAD_W:(hd + 1) * HEAD_W, :] = yt
        ktb_ref[hd * HEAD_W:(hd + 1) * HEAD_W, :] = yt.astype(BF16)
        kb_ref[:, hd * HEAD_W:(hd + 1) * HEAD_W] = y.astype(BF16)

    qk_norm(group(0), qg_ref[...], emit_q)
    qk_norm(group(1), kg_ref[...], emit_k)
    va = group(2)
    vf_ref[...] = va
    vb_ref[...] = va.astype(BF16)
    z = group(3)
    half = D_MODEL // 2
    qr_ref[...] = z[:, :half].astype(BF16)
    kr_ref[...] = (z[:, half:] * QK_SCALE).astype(BF16)
    vr_ref[...] = group(4).astype(BF16)
    gr_ref[...] = group(5)
    ga_ref[...] = group(6)
    gb_ref[...] = group(7)


def _proj(x, mods, g1, w_in, qg, kg, *, tm, rows_per_batch=None, stacked=None, layer=None):
    t, d = x.shape
    row = lambda w: pl.BlockSpec((tm, w), lambda i: (i, 0))
    const = lambda shp: pl.BlockSpec(shp, lambda i: (0,) * len(shp))
    sds = jax.ShapeDtypeStruct
    if stacked is None:
        mod_spec = pl.BlockSpec((6, tm, d), lambda i: (0, i, 0))
        kt_spec = pl.BlockSpec((d, tm), lambda i: (0, i))
        ktf = (kt_spec, sds((d, t), F32))
        ktb = (kt_spec, sds((d, t), BF16))
        vf = (row(d), sds((t, d), F32))
        extra_in, extra_specs, aliases = [], [], {}
    else:
        k_all, v_all = stacked
        s = rows_per_batch
        tpb = s // tm
        mod_spec = pl.BlockSpec((6, 1, 1, d), lambda i: (0, i // tpb, 0, 0))
        ktf = (pl.BlockSpec((None, None, d, tm), lambda i: (layer, i // tpb, 0, i % tpb)),
               sds(k_all.shape, F32))
        ktb = (pl.BlockSpec((None, d, tm), lambda i: (i // tpb, 0, i % tpb)), sds((t // s, d, s), BF16))
        vf = (pl.BlockSpec((None, tm, d), lambda i: (layer, i, 0)), sds(v_all.shape, F32))
        extra_in = [k_all, v_all]
        extra_specs = [pl.BlockSpec(memory_space=pl.ANY)] * 2
        aliases = {6: 1, 7: 4}
    plain = lambda w, dt: (row(w), sds((t, w), dt))
    outs = [plain(d, BF16), ktf, ktb, plain(d, BF16), vf, plain(d, BF16),
            plain(d // 2, BF16), plain(d // 2, BF16), plain(d, BF16),
            plain(d, F32), plain(d, F32), plain(d, F32)]
    return pl.pallas_call(
        functools.partial(_proj_kernel, per_row_mod=stacked is None, n_aliased=len(extra_in)),
        grid=(t // tm,),
        in_specs=[
            row(d), mod_spec, const((1, d)),
            pl.BlockSpec(w_in.shape, lambda i: (0, 0), pipeline_mode=pl.Buffered(1)),
            const((1, HEAD_W)), const((1, HEAD_W)),
        ] + extra_specs,
        out_specs=[o[0] for o in outs],
        out_shape=[o[1] for o in outs],
        input_output_aliases=aliases,
        compiler_params=_cparams("arbitrary"),
        name="proj",
    )(x, mods, g1, w_in, qg, kg, *extra_in)


def _lambda(lq1, lk1, lq2, lk2, lam_init):
    a = jnp.sum(lq1 * lk1, axis=-1, keepdims=True)
    b = jnp.sum(lq2 * lk2, axis=-1, keepdims=True)
    return jnp.exp(a) - jnp.exp(b) + lam_init


BIAS_ROWS = 16
BIAS_TERMS = 3


def _attn_kernel(slope_ref, q_ref, kt_ref, v_ref, lq1_ref, lk1_ref, lq2_ref, lk2_ref, o_ref,
                 kaug_scr, vaug_scr, qq_scr, sa_scr, sb_scr, m_scr, l_scr, acc_scr, *, tq, lam_init):
    hd = pl.program_id(1)
    tk = tq // 2
    s_len = kt_ref.shape[-1]
    n_q = s_len // tq

    kaug_scr[:HEAD_W, :] = kt_ref[0]
    r = lax.broadcasted_iota(jnp.int32, (BIAS_ROWS, s_len), 0)
    x = lax.broadcasted_iota(jnp.int32, (BIAS_ROWS, s_len), 1).astype(F32) * slope_ref[hd]
    hi = x.astype(BF16).astype(F32)
    mid = (x - hi).astype(BF16).astype(F32)
    lo = x - hi - mid
    pieces = jnp.where(r == 0, hi, jnp.where(r == 1, mid, jnp.where(r == 2, lo, 0.0)))
    kaug_scr[HEAD_W:HEAD_W + BIAS_ROWS, :] = pieces.astype(BF16)
    kaug_scr[HEAD_W + BIAS_ROWS:, :] = jnp.zeros((HEAD_W - BIAS_ROWS, s_len), BF16)
    vaug_scr[:, :HEAD_W] = v_ref[0]
    vaug_scr[:, HEAD_W:] = jnp.ones((s_len, HEAD_W), BF16)

    lane = lax.broadcasted_iota(jnp.int32, (tq, HEAD_W), 1)
    ones_cols = jnp.where(lane < BIAS_TERMS, 1.0, 0.0).astype(BF16)
    qq_scr[:tq, HEAD_W:] = ones_cols
    qq_scr[tq:, HEAD_W:] = ones_cols
    lam = _lambda(lq1_ref[...], lk1_ref[...], lq2_ref[...], lk2_ref[...], lam_init)

    def load_q(i):
        q = q_ref[0, pl.ds(pl.multiple_of(i * tq, tq), tq), :]
        zero = jnp.zeros_like(q)
        qq_scr[:tq, :HEAD_W] = jnp.where(lane < DK, q, zero)
        qq_scr[tq:, :HEAD_W] = jnp.where(lane >= DK, q, zero)

    def logits(j, s_ref):
        start = pl.multiple_of(j * tk, tk)
        s_ref[...] = _dot(qq_scr[...], kaug_scr[:, pl.ds(start, tk)])

    tri_row = lax.broadcasted_iota(jnp.int32, (tk, tk), 0)
    tri_col = lax.broadcasted_iota(jnp.int32, (tk, tk), 1)

    def causal(blk):
        return jnp.where(tri_col <= tri_row, blk, NEG_INF)

    def absorb(j, s_ref, diag=None):
        start = pl.multiple_of(j * tk, tk)
        s = s_ref[...]
        if diag == 0:
            s = jnp.concatenate([causal(s[:tk]), s[tk:tq], causal(s[tq:tq + tk]), s[tq + tk:]], axis=0)
        elif diag == 1:
            dark = jnp.full((tk, tk), NEG_INF, F32)
            s = jnp.concatenate([dark, causal(s[tk:tq]), dark, causal(s[tq + tk:])], axis=0)
        m_prev = m_scr[...]
        m_new = jnp.maximum(m_prev, jnp.max(s, axis=-1, keepdims=True))
        alpha = jnp.exp2(m_prev - m_new)
        p = jnp.exp2(s - jnp.tile(m_new, (1, tk // HEAD_W)))
        pv = _dot(p.astype(BF16), vaug_scr[pl.ds(start, tk), :])
        l_scr[...] = alpha * l_scr[...] + pv[:, HEAD_W:]
        acc_scr[...] = alpha * acc_scr[...] + pv[:, :HEAD_W]
        m_scr[...] = m_new

    def query_tile(i, carry):
        m_scr[...] = jnp.full(m_scr.shape, NEG_INF, F32)
        l_scr[...] = jnp.zeros(l_scr.shape, F32)
        acc_scr[...] = jnp.zeros(acc_scr.shape, F32)

        def pair(t, c):
            j = 2 * t
            logits(j + 1, sb_scr)
            absorb(j, sa_scr)
            logits(j + 2, sa_scr)
            absorb(j + 1, sb_scr)
            return c

        lax.fori_loop(0, i, pair, 0)
        logits(2 * i + 1, sb_scr)
        absorb(2 * i, sa_scr, diag=0)
        load_q(jnp.minimum(i + 1, n_q - 1))
        logits(0, sa_scr)
        absorb(2 * i + 1, sb_scr, diag=1)

        o = acc_scr[...] / l_scr[...]
        o_ref[0, pl.ds(pl.multiple_of(i * tq, tq), tq), :] = o[:tq] - lam * o[tq:]
        return carry

    load_q(0)
    logits(0, sa_scr)
    lax.fori_loop(0, n_q, query_tile, 0)


def _attn_prompt(q, kt, v, lam_vecs, lam_init, *, tq):
    b, s, d = q.shape
    slopes = jnp.asarray(LOG2E * np.exp2(-np.arange(1, N_HEADS + 1, dtype=np.float64)), F32)
    vec = pl.BlockSpec((1, DK), lambda bb, h: (0, 0))
    rows = pl.BlockSpec((1, s, HEAD_W), lambda bb, h: (bb, 0, h))
    return pl.pallas_call(
        functools.partial(_attn_kernel, tq=tq, lam_init=lam_init),
        grid=(b, N_HEADS),
        in_specs=[
            pl.BlockSpec(memory_space=pltpu.SMEM),
            rows,
            pl.BlockSpec((1, HEAD_W, s), lambda bb, h: (bb, h, 0)),
            rows,
            vec, vec, vec, vec,
        ],
        out_specs=rows,
        out_shape=jax.ShapeDtypeStruct((b, s, d), F32),
        scratch_shapes=[pltpu.VMEM((2 * HEAD_W, s), BF16), pltpu.VMEM((s, 2 * HEAD_W), BF16),
                        pltpu.VMEM((2 * tq, 2 * HEAD_W), BF16)]
                       + [pltpu.VMEM((2 * tq, tq // 2), F32)] * 2
                       + [pltpu.VMEM((2 * tq, HEAD_W), F32)] * 3,
        compiler_params=_cparams("arbitrary", "arbitrary"),
        name="attn_prompt",
    )(slopes, q, kt, v, *lam_vecs)


def _ret_tables(chunk):
    log_g = np.log1p(-np.exp2(-5.0 - np.arange(N_HEADS, dtype=np.float64)))
    idx = np.arange(chunk, dtype=np.float64)
    diff = idx[:, None] - idx[None, :]
    dec = np.where(diff >= 0, np.exp(log_g[:, None, None] * np.maximum(diff, 0.0)), 0.0)
    cross = np.exp(log_g[:, None] * (idx[None, :] + 1.0))
    kdec = np.exp(log_g[:, None] * (chunk - 1.0 - idx[None, :]))
    rep = lambda a: np.repeat(a[:, :, None], HEAD_W, axis=2)
    return (jnp.asarray(dec, F32), jnp.asarray(rep(cross), F32), jnp.asarray(rep(kdec), F32),
            jnp.asarray(np.exp(log_g * chunk), F32))


def _ret_kernel(gl_ref, q_ref, k_ref, v_ref, dec_ref, cross_ref, kdec_ref, o_ref, fin_ref, st_scr):
    pair = pl.program_id(1)
    c = pl.program_id(2)

    @pl.when(c == 0)
    def _():
        st_scr[...] = jnp.zeros(st_scr.shape, F32)

    qb = q_ref[0]
    kb = k_ref[0]
    lane = lax.broadcasted_iota(jnp.int32, qb.shape, 1)
    zero = jnp.zeros_like(qb)
    for hh in range(2):
        msk = (lane < DK) if hh == 0 else (lane >= DK)
        qh = jnp.where(msk, qb, zero)
        kh = jnp.where(msk, kb, zero)
        vh = v_ref[0, :, hh * HEAD_W:(hh + 1) * HEAD_W]
        st = st_scr[hh]
        inner = _dot_nt(qh, kh) * dec_ref[hh]
        o = _dot(inner.astype(BF16), vh) + _dot(qh, st.astype(BF16)) * cross_ref[hh]
        o_ref[0, :, hh * HEAD_W:(hh + 1) * HEAD_W] = o
        kd = (kh.astype(F32) * kdec_ref[hh]).astype(BF16)
        st_scr[hh] = st * gl_ref[2 * pair + hh] + _dot_tn(kd, vh)

    @pl.when(c == pl.num_programs(2) - 1)
    def _():
        fin_ref[0, 0] = st_scr[0, :DK, :]
        fin_ref[0, 1] = st_scr[1, DK:, :]


def _ret_prompt(qr, kr, vr, *, chunk):
    b, s, d = vr.shape
    dec, cross, kdec, gl = _ret_tables(chunk)
    qk = pl.BlockSpec((1, chunk, HEAD_W), lambda bb, p, c: (bb, c, p))
    vo = pl.BlockSpec((1, chunk, 2 * HEAD_W), lambda bb, p, c: (bb, c, p))
    tab = lambda w: pl.BlockSpec((2, chunk, w), lambda bb, p, c: (p, 0, 0))
    return pl.pallas_call(
        _ret_kernel,
        grid=(b, N_HEADS // 2, s // chunk),
        in_specs=[pl.BlockSpec(memory_space=pltpu.SMEM), qk, qk, vo,
                  tab(chunk), tab(HEAD_W), tab(HEAD_W)],
        out_specs=[vo, pl.BlockSpec((1, 2, DK, HEAD_W), lambda bb, p, c: (bb, p, 0, 0))],
        out_shape=[jax.ShapeDtypeStruct((b, s, d), F32),
                   jax.ShapeDtypeStruct((b, N_HEADS, DK, HEAD_W), F32)],
        scratch_shapes=[pltpu.VMEM((2, HEAD_W, HEAD_W), F32)],
        compiler_params=_cparams("arbitrary", "arbitrary", "arbitrary"),
        name="ret_prompt",
    )(gl, qr, kr, vr, dec, cross, kdec)


def _head_norm(t, g):
    parts = []
    for hd in range(N_HEADS):
        tc = t[:, hd * HEAD_W:(hd + 1) * HEAD_W]
        ms = jnp.mean(tc * tc, axis=-1, keepdims=True)
        parts.append(tc * lax.rsqrt(ms + EPS) * g)
    return jnp.concatenate(parts, axis=-1)


def _merge_mlp_kernel(x_ref, oa_ref, or_ref, gr_ref, ga_ref, gb_ref, mod_ref, sg_ref, rg_ref, n2_ref,
                      wo_ref, wu_ref, wd_ref, y_ref, *, per_row_mod, lam_init):
    if per_row_mod:
        gt_a, sh_m, sc_m, gt_m = (mod_ref[c] for c in (2, 3, 4, 5))
    else:
        gt_a, sh_m, sc_m, gt_m = (mod_ref[c, 0] for c in (2, 3, 4, 5))
    ya = _head_norm(oa_ref[...], sg_ref[...]) * (1.0 - lam_init)
    yr = _head_norm(or_ref[...], rg_ref[...]) * _silu(gr_ref[...])
    mixed = jax.nn.sigmoid(ga_ref[...]) * ya + jax.nn.sigmoid(gb_ref[...]) * yr
    x1 = x_ref[...] + gt_a * _dot(mixed.astype(BF16), wo_ref[...])
    h2 = _modulated_norm(x1, n2_ref[...], sh_m, sc_m).astype(BF16)
    u = jnp.maximum(_dot(h2, wu_ref[...]), 0.0)
    y_ref[...] = x1 + gt_m * _dot((u * u).astype(BF16), wd_ref[...])


def _merge_mlp(x, oa, orr, gr, ga, gb, mods, sg, rg, n2, w_out, w_up, w_down, lam_init, *, tm,
               rows_per_batch):
    t, d = x.shape
    per_row = rows_per_batch is None
    if per_row:
        mod_spec = pl.BlockSpec((6, tm, d), lambda i: (0, i, 0))
    else:
        tiles_per_batch = rows_per_batch // tm
        mod_spec = pl.BlockSpec((6, 1, 1, d), lambda i: (0, i // tiles_per_batch, 0, 0))
    row = pl.BlockSpec((tm, d), lambda i: (i, 0))
    const = lambda shp: pl.BlockSpec(shp, lambda i: (0,) * len(shp))
    weight = lambda w: pl.BlockSpec(w.shape, lambda i: (0, 0), pipeline_mode=pl.Buffered(1))
    return pl.pallas_call(
        functools.partial(_merge_mlp_kernel, per_row_mod=per_row, lam_init=lam_init),
        grid=(t // tm,),
        in_specs=[row] * 6 + [mod_spec, const((1, HEAD_W)), const((1, HEAD_W)), const((1, d)),
                               weight(w_out), weight(w_up), weight(w_down)],
        out_specs=row,
        out_shape=jax.ShapeDtypeStruct((t, d), F32),
        compiler_params=_cparams("arbitrary"),
        name="merge_mlp",
    )(x, oa, orr, gr, ga, gb, mods, sg, rg, n2, w_out, w_up, w_down)


def _attn_decode_kernel(pt_ref, slope_ref, expand_ref, q_ref, kn_ref, vn_ref,
                        lq1_ref, lk1_ref, lq2_ref, lk2_ref, *rest, pages_per_step, page, past, lam_init):
    k_refs = rest[:pages_per_step]
    v_refs = rest[pages_per_step:2 * pages_per_step]
    o_ref, kt_scr, v_scr, m_scr, l_scr, acc_scr = rest[2 * pages_per_step:]
    g = pl.program_id(1)
    nrow = 2 * N_HEADS
    d = D_MODEL

    q = q_ref[0].astype(F32)
    row = lax.broadcasted_iota(jnp.int32, (nrow, d), 0)
    lane = lax.broadcasted_iota(jnp.int32, (nrow, d), 1)
    own = (lane // DK) == 2 * (row % N_HEADS) + row // N_HEADS
    qbd = jnp.where(own, jnp.broadcast_to(q, (nrow, d)), 0.0)
    c2 = slope_ref[...]

    @pl.when(g == 0)
    def _():
        kn = kn_ref[0].astype(F32)
        m0 = jnp.sum(qbd * kn, axis=-1, keepdims=True)
        m_scr[...] = jnp.broadcast_to(m0, m_scr.shape)
        l_scr[...] = jnp.ones(l_scr.shape, F32)
        v8 = vn_ref[0].astype(F32)
        acc_scr[...] = jnp.concatenate([v8, v8], axis=0)

    for t in range(pages_per_step):
        kt_scr[:, t * page:(t + 1) * page] = k_refs[t][0, 0].reshape(d, page).astype(BF16)
        v_scr[t // 2, :, (t % 2) * HEAD_W:(t % 2 + 1) * HEAD_W] = (
            v_refs[t][0, 0].reshape(page * N_HEADS, HEAD_W).astype(BF16))
    span = pages_per_step * page
    kpos = g * span + lax.broadcasted_iota(jnp.int32, (1, span), 1)
    s = _dot(qbd.astype(BF16), kt_scr[...]) + (kpos - past).astype(F32) * c2
    m_prev = m_scr[...]
    m_new = jnp.maximum(m_prev, jnp.max(s, axis=-1, keepdims=True))
    alpha = jnp.exp2(m_prev - m_new)
    p = jnp.exp2(s - m_new[:, :1])
    l_new = alpha * l_scr[...] + jnp.sum(p, axis=-1, keepdims=True)
    by_page = jnp.concatenate([p[:, t * page:(t + 1) * page] for t in range(pages_per_step)], axis=0)
    spread = _dot(by_page.astype(BF16), expand_ref[...])
    srow = lax.broadcasted_iota(jnp.int32, spread.shape, 0)
    scol = lax.broadcasted_iota(jnp.int32, spread.shape, 1)
    pe = jnp.where(scol % N_HEADS == srow % N_HEADS, spread, 0.0).astype(BF16)
    acc = alpha * acc_scr[...]
    for u in range(pages_per_step // 2):
        pv = _dot(pe[2 * nrow * u:2 * nrow * (u + 1)], v_scr[u])
        acc = acc + pv[:nrow, :HEAD_W] + pv[nrow:, HEAD_W:]
    m_scr[...] = m_new
    l_scr[...] = l_new
    acc_scr[...] = acc

    @pl.when(g == pl.num_programs(1) - 1)
    def _():
        lam = _lambda(lq1_ref[...], lk1_ref[...], lq2_ref[...], lk2_ref[...], lam_init)
        o = acc / l_new
        o_ref[0] = o[:N_HEADS] - lam * o[N_HEADS:]


def _attn_decode(q, k_new, v_new, cache_kt, cache_v, layer, page_table, lam_vecs, lam_init, *,
                 pages_per_step):
    n, _, d = q.shape
    n_pages = page_table.shape[1]
    page = cache_v.shape[2]
    steps = n_pages // pages_per_step
    slopes = np.tile(LOG2E * np.exp2(-np.arange(1, N_HEADS + 1, dtype=np.float64)), 2)
    slopes = jnp.asarray(np.repeat(slopes[:, None], pages_per_step * page, axis=1), F32)
    expand = jnp.asarray(np.repeat(np.eye(page), N_HEADS, axis=1), BF16)
    tok = pl.BlockSpec((1, 1, d), lambda b, g, pt: (b, 0, 0))
    hv = pl.BlockSpec((1, N_HEADS, HEAD_W), lambda b, g, pt: (b, 0, 0))
    vec = pl.BlockSpec((1, DK), lambda b, g, pt: (0, 0))
    const = lambda a: pl.BlockSpec(a.shape, lambda b, g, pt: (0, 0))

    def phys(b, g, pt, t):
        return pt[b * n_pages + g * pages_per_step + t]

    def k_spec(t):
        return pl.BlockSpec((1, 1, N_HEADS, 2, DK, page),
                            lambda b, g, pt: (layer, phys(b, g, pt, t), 0, 0, 0, 0))

    def v_spec(t):
        return pl.BlockSpec((1, 1, page, N_HEADS, HEAD_W),
                            lambda b, g, pt: (layer, phys(b, g, pt, t), 0, 0, 0))

    kernel = functools.partial(_attn_decode_kernel, pages_per_step=pages_per_step, page=page,
                               past=n_pages * page, lam_init=lam_init)
    return pl.pallas_call(
        kernel,
        grid_spec=pltpu.PrefetchScalarGridSpec(
            num_scalar_prefetch=1,
            grid=(n, steps),
            in_specs=[const(slopes), const(expand), tok, tok, hv, vec, vec, vec, vec]
                     + [k_spec(t) for t in range(pages_per_step)]
                     + [v_spec(t) for t in range(pages_per_step)],
            out_specs=hv,
            scratch_shapes=[pltpu.VMEM((d, pages_per_step * page), BF16),
                            pltpu.VMEM((pages_per_step // 2, page * N_HEADS, 2 * HEAD_W), BF16)]
                           + [pltpu.VMEM((2 * N_HEADS, HEAD_W), F32)] * 3,
        ),
        out_shape=jax.ShapeDtypeStruct((n, N_HEADS, HEAD_W), F32),
        compiler_params=_cparams("arbitrary", "arbitrary"),
        name="attn_decode",
    )(page_table.reshape(-1), slopes, expand, q, k_new, v_new, *lam_vecs,
      *([cache_kt] * pages_per_step), *([cache_v] * pages_per_step))


def _ret_step_kernel(q_ref, k_ref, v_ref, gam_ref, s_ref, o_ref, sn_ref, *, bb):
    rows = N_HEADS * DK
    eye = (lax.broadcasted_iota(jnp.int32, (rows, rows), 0)
           == lax.broadcasted_iota(jnp.int32, (rows, rows), 1))
    gam = gam_ref[...]

    def to_col(r):
        return jnp.sum(jnp.where(eye, jnp.broadcast_to(r, (rows, rows)), 0.0), axis=-1, keepdims=True)

    def head_sum(a):
        return jnp.sum(a.reshape(N_HEADS, DK, a.shape[-1]), axis=1)

    for i in range(bb):
        qcol = to_col(q_ref[i:i + 1, :])
        kcol = to_col(k_ref[i:i + 1, :])
        v8 = v_ref[i]
        vexp = jnp.concatenate(
            [jnp.broadcast_to(v8[hd:hd + 1, :], (DK, HEAD_W)) for hd in range(N_HEADS)], axis=0)
        s0 = s_ref[0, i].reshape(rows, HEAD_W)
        qk = head_sum(jnp.broadcast_to(qcol * kcol, (rows, HEAD_W)))
        o_ref[i] = head_sum(s0 * gam * qcol) + qk * v8
        sn_ref[i] = (s0 * gam + kcol * vexp).reshape(N_HEADS, DK, HEAD_W)


def _ret_step(qr, kr, vr, state, layer, *, bb):
    n = qr.shape[0]
    gamma = 1.0 - np.exp2(-5.0 - np.arange(N_HEADS, dtype=np.float64))
    gam = jnp.asarray(np.repeat(np.repeat(gamma, DK)[:, None], HEAD_W, axis=1), F32)
    qk = pl.BlockSpec((bb, N_HEADS * DK), lambda i: (i, 0))
    hv = pl.BlockSpec((bb, N_HEADS, HEAD_W), lambda i: (i, 0, 0))
    return pl.pallas_call(
        functools.partial(_ret_step_kernel, bb=bb),
        grid=(n // bb,),
        in_specs=[qk, qk, hv, pl.BlockSpec(gam.shape, lambda i: (0, 0)),
                  pl.BlockSpec((1, bb, N_HEADS, DK, HEAD_W), lambda i: (layer, i, 0, 0, 0))],
        out_specs=[hv, pl.BlockSpec((bb, N_HEADS, DK, HEAD_W), lambda i: (i, 0, 0, 0))],
        out_shape=[jax.ShapeDtypeStruct((n, N_HEADS, HEAD_W), F32),
                   jax.ShapeDtypeStruct((n, N_HEADS, DK, HEAD_W), F32)],
        compiler_params=_cparams("arbitrary"),
        name="ret_step",
    )(qr, kr, vr, gam, state)


def kernel(x_prompt, x_sample, cache_k, cache_v, state_ret, page_table, c_prompt, c_sample, ada_w, ada_b, norm1_g, norm2_g, w_in, q_norm_g, k_norm_g, lam_q1, lam_k1, lam_q2, lam_k2, subln_g, ret_norm_g, w_out, w_up, w_down):
    depth = ada_w.shape[0]
    bp, s, d = x_prompt.shape
    ns = x_sample.shape[0]
    ckt = jnp.transpose(cache_k, (0, 1, 3, 4, 5, 2))

    bp_pad = -(-bp // SUBLANES) * SUBLANES
    c_p = jnp.pad(c_prompt, ((0, bp_pad - bp), (0, 0)))
    mods_p, mods_s = _adaln(c_p, c_sample, ada_w, ada_b)
    mods_p = mods_p.reshape(depth, 6, bp_pad, 1, d)
    w_in_b, w_out_b = w_in.astype(BF16), w_out.astype(BF16)
    w_up_b, w_down_b = w_up.astype(BF16), w_down.astype(BF16)

    xp = x_prompt.reshape(bp * s, d)
    xs = x_sample.reshape(ns, d)
    k_all = jnp.zeros((depth, bp, d, s), F32)
    v_all = jnp.zeros((depth, bp * s, d), F32)
    sp_l, ks_l, vs_l, ss_l = [], [], [], []
    for l in range(depth):
        lam_init = 0.8 - 0.6 * math.exp(-0.3 * l)
        lam_vecs = [a[l].reshape(1, DK) for a in (lam_q1, lam_k1, lam_q2, lam_k2)]
        g1, n2 = norm1_g[l].reshape(1, d), norm2_g[l].reshape(1, d)
        qg = jnp.tile(q_norm_g[l], 2).reshape(1, HEAD_W)
        kg = jnp.tile(k_norm_g[l], 2).reshape(1, HEAD_W)
        sg, rg = subln_g[l].reshape(1, HEAD_W), ret_norm_g[l].reshape(1, HEAD_W)

        q, k_all, ktb, _, v_all, vb, qr, kr, vr, gr, ga, gb = _proj(
            xp, mods_p[l], g1, w_in_b[l], qg, kg, tm=256, rows_per_batch=s,
            stacked=(k_all, v_all), layer=l)
        b3 = lambda a: a.reshape(bp, s, a.shape[-1])
        oa = _attn_prompt(b3(q), ktb, b3(vb), lam_vecs, lam_init, tq=1024)
        orr, s_fin = _ret_prompt(b3(qr), b3(kr), b3(vr), chunk=512)
        xp = _merge_mlp(xp, oa.reshape(bp * s, d), orr.reshape(bp * s, d), gr, ga, gb, mods_p[l],
                        sg, rg, n2, w_out_b[l], w_up_b[l], w_down_b[l], lam_init,
                        tm=256, rows_per_batch=s)
        sp_l.append(s_fin)

        q, ktf, _, kb, vf, vb, qr, kr, vr, gr, ga, gb = _proj(
            xs, mods_s[l], g1, w_in_b[l], qg, kg, tm=ns)
        t3 = lambda a: a.reshape(ns, 1, d)
        h3 = lambda a: a.reshape(ns, N_HEADS, HEAD_W)
        oa = _attn_decode(t3(q), t3(kb), h3(vb), ckt, cache_v, l, page_table, lam_vecs, lam_init,
                          pages_per_step=8)
        orr, s_new = _ret_step(qr.astype(F32), kr.astype(F32), h3(vr.astype(F32)), state_ret, l, bb=8)
        xs = _merge_mlp(xs, oa.reshape(ns, d), orr.reshape(ns, d), gr, ga, gb, mods_s[l],
                        sg, rg, n2, w_out_b[l], w_up_b[l], w_down_b[l], lam_init,
                        tm=ns, rows_per_batch=None)
        ks_l.append(ktf)
        vs_l.append(vf)
        ss_l.append(s_new)

    h2 = (N_HEADS, 2, DK)
    k_prompt = jnp.transpose(k_all.reshape(depth, bp, *h2, s), (0, 1, 5, 2, 3, 4))
    k_sample = jnp.transpose(jnp.stack(ks_l).reshape(depth, *h2, ns, 1), (0, 4, 5, 1, 2, 3))
    return (xp.reshape(bp, s, d), xs.reshape(ns, 1, d),
            k_prompt,
            v_all.reshape(depth, bp, s, N_HEADS, HEAD_W),
            jnp.stack(sp_l),
            k_sample,
            jnp.stack(vs_l).reshape(depth, ns, 1, N_HEADS, HEAD_W),
            jnp.stack(ss_l))
```

```python
import functools
import math

import numpy as np
import jax
import jax.numpy as jnp
from jax import lax
from jax.experimental import pallas as pl
from jax.experimental.pallas import tpu as pltpu

F32 = jnp.float32
BF16 = jnp.bfloat16

D_MODEL = 1024
N_HEADS = 8
HEAD_W = 128
SUBLANES = 8
DK = 64
N_GROUPS = 8
EPS = 1e-6
NEG_INF = -1e30
LOG2E = 1.4426950408889634
QK_SCALE = DK ** -0.5

VMEM_LIMIT = 56 * 1024 * 1024


def _cparams(*sem):
    return pltpu.CompilerParams(dimension_semantics=sem, vmem_limit_bytes=VMEM_LIMIT)


def _silu(x):
    return x * jax.nn.sigmoid(x)


def _dot(a, b):
    return jnp.dot(a, b, preferred_element_type=F32)


def _dot_nt(a, b):
    return lax.dot_general(a, b, (((1,), (1,)), ((), ())), preferred_element_type=F32)


def _dot_tn(a, b):
    return lax.dot_general(a, b, (((0,), (0,)), ((), ())), preferred_element_type=F32)


def _adaln_kernel(cp_ref, cs_ref, w_ref, b_ref, op_ref, os_ref):
    w = w_ref[0].astype(BF16)
    b = b_ref[0, 0]
    op_ref[0, 0] = _dot(_silu(cp_ref[...]).astype(BF16), w) + b
    os_ref[0, 0] = _dot(_silu(cs_ref[...]).astype(BF16), w) + b


def _adaln(c_p, c_s, ada_w, ada_b):
    depth = ada_w.shape[0]
    bp, bs = c_p.shape[0], c_s.shape[0]
    d = D_MODEL
    b4 = ada_b.reshape(depth, 6, 1, d)
    return pl.pallas_call(
        _adaln_kernel,
        grid=(depth, 6),
        in_specs=[
            pl.BlockSpec((bp, d), lambda l, j: (0, 0)),
            pl.BlockSpec((bs, d), lambda l, j: (0, 0)),
            pl.BlockSpec((1, d, d), lambda l, j: (l, 0, j)),
            pl.BlockSpec((1, 1, 1, d), lambda l, j: (l, j, 0, 0)),
        ],
        out_specs=[
            pl.BlockSpec((1, 1, bp, d), lambda l, j: (l, j, 0, 0)),
            pl.BlockSpec((1, 1, bs, d), lambda l, j: (l, j, 0, 0)),
        ],
        out_shape=[
            jax.ShapeDtypeStruct((depth, 6, bp, d), F32),
            jax.ShapeDtypeStruct((depth, 6, bs, d), F32),
        ],
        compiler_params=_cparams("arbitrary", "arbitrary"),
        name="adaln",
    )(c_p, c_s, ada_w, b4)


def _modulated_norm(x, g, shift, scale):
    ms = jnp.mean(x * x, axis=-1, keepdims=True)
    return (x * lax.rsqrt(ms + EPS) * g) * (1.0 + scale) + shift


def _proj_kernel(x_ref, mod_ref, g1_ref, w_ref, qg_ref, kg_ref, *rest, per_row_mod, n_aliased):
    (q_ref, ktf_ref, ktb_ref, kb_ref, vf_ref, vb_ref, qr_ref, kr_ref, vr_ref,
     gr_ref, ga_ref, gb_ref) = rest[n_aliased:]
    if per_row_mod:
        shift, scale = mod_ref[0], mod_ref[1]
    else:
        shift, scale = mod_ref[0, 0], mod_ref[1, 0]
    h = _modulated_norm(x_ref[...], g1_ref[...], shift, scale).astype(BF16)

    def group(c):
        return _dot(h, w_ref[:, c * D_MODEL:(c + 1) * D_MODEL])

    lo = lax.broadcasted_iota(jnp.int32, (1, HEAD_W), 1) < DK

    def qk_norm(z, g, emit):
        for hd in range(N_HEADS):
            zc = z[:, hd * HEAD_W:(hd + 1) * HEAD_W]
            sq = zc * zc
            s_lo = jnp.sum(jnp.where(lo, sq, 0.0), axis=-1, keepdims=True)
            s_all = jnp.sum(sq, axis=-1, keepdims=True)
            ms = jnp.where(lo, s_lo, s_all - s_lo) * (1.0 / DK)
            emit(hd, zc * lax.rsqrt(ms + EPS) * g)

    def emit_q(hd, y):
        q_ref[:, hd * HEAD_W:(hd + 1) * HEAD_W] = (y * (QK_SCALE * LOG2E)).astype(BF16)

    def emit_k(hd, y):
        yt = y.T
        ktf_ref[hd * HEAD_W:(hd + 1) * HEAD_W, :] = yt
        ktb_ref[hd * HEAD_W:(hd + 1) * HEAD_W, :] = yt.astype(BF16)
        kb_ref[:, hd * HEAD_W:(hd + 1) * HEAD_W] = y.astype(BF16)

    qk_norm(group(0), qg_ref[...], emit_q)
    qk_norm(group(1), kg_ref[...], emit_k)
    va = group(2)
    vf_ref[...] = va
    vb_ref[...] = va.astype(BF16)
    z = group(3)
    half = D_MODEL // 2
    qr_ref[...] = z[:, :half].astype(BF16)
    kr_ref[...] = (z[:, half:] * QK_SCALE).astype(BF16)
    vr_ref[...] = group(4).astype(BF16)
    gr_ref[...] = group(5)
    ga_ref[...] = group(6)
    gb_ref[...] = group(7)


def _proj(x, mods, g1, w_in, qg, kg, *, tm, rows_per_batch=None, stacked=None, layer=None):
    t, d = x.shape
    row = lambda w: pl.BlockSpec((tm, w), lambda i: (i, 0))
    const = lambda shp: pl.BlockSpec(shp, lambda i: (0,) * len(shp))
    sds = jax.ShapeDtypeStruct
    if stacked is None:
        mod_spec = pl.BlockSpec((6, tm, d), lambda i: (0, i, 0))
        kt_spec = pl.BlockSpec((d, tm), lambda i: (0, i))
        ktf = (kt_spec, sds((d, t), F32))
        ktb = (kt_spec, sds((d, t), BF16))
        vf = (row(d), sds((t, d), F32))
        extra_in, extra_specs, aliases = [], [], {}
    else:
        k_all, v_all = stacked
        s = rows_per_batch
        tpb = s // tm
        mod_spec = pl.BlockSpec((6, 1, 1, d), lambda i: (0, i // tpb, 0, 0))
        ktf = (pl.BlockSpec((None, None, d, tm), lambda i: (layer, i // tpb, 0, i % tpb)),
               sds(k_all.shape, F32))
        ktb = (pl.BlockSpec((None, d, tm), lambda i: (i // tpb, 0, i % tpb)), sds((t // s, d, s), BF16))
        vf = (pl.BlockSpec((None, tm, d), lambda i: (layer, i, 0)), sds(v_all.shape, F32))
        extra_in = [k_all, v_all]
        extra_specs = [pl.BlockSpec(memory_space=pl.ANY)] * 2
        aliases = {6: 1, 7: 4}
    plain = lambda w, dt: (row(w), sds((t, w), dt))
    outs = [plain(d, BF16), ktf, ktb, plain(d, BF16), vf, plain(d, BF16),
            plain(d // 2, BF16), plain(d // 2, BF16), plain(d, BF16),
            plain(d, F32), plain(d, F32), plain(d, F32)]
    return pl.pallas_call(
        functools.partial(_proj_kernel, per_row_mod=stacked is None, n_aliased=len(extra_in)),
        grid=(t // tm,),
        in_specs=[
            row(d), mod_spec, const((1, d)),
            pl.BlockSpec((None,) + w_in.shape[1:], lambda i: (layer, 0, 0), pipeline_mode=pl.Buffered(1)),
            const((1, HEAD_W)), const((1, HEAD_W)),
        ] + extra_specs,
        out_specs=[o[0] for o in outs],
        out_shape=[o[1] for o in outs],
        input_output_aliases=aliases,
        compiler_params=_cparams("arbitrary"),
        name="proj",
    )(x, mods, g1, w_in, qg, kg, *extra_in)


def _lambda(lq1, lk1, lq2, lk2, lam_init):
    a = jnp.sum(lq1 * lk1, axis=-1, keepdims=True)
    b = jnp.sum(lq2 * lk2, axis=-1, keepdims=True)
    return jnp.exp(a) - jnp.exp(b) + lam_init


BIAS_ROWS = 16
BIAS_TERMS = 3


def _attn_kernel(slope_ref, q_ref, kt_ref, v_ref, lq1_ref, lk1_ref, lq2_ref, lk2_ref, o_ref,
                 kaug_scr, vaug_scr, qq_scr, sa_scr, sb_scr, m_scr, l_scr, acc_scr, *, tq, lam_init):
    hd = pl.program_id(1)
    tk = tq // 2
    s_len = kt_ref.shape[-1]
    n_q = s_len // tq

    kaug_scr[:HEAD_W, :] = kt_ref[0]
    r = lax.broadcasted_iota(jnp.int32, (BIAS_ROWS, s_len), 0)
    x = lax.broadcasted_iota(jnp.int32, (BIAS_ROWS, s_len), 1).astype(F32) * slope_ref[hd]
    hi = x.astype(BF16).astype(F32)
    mid = (x - hi).astype(BF16).astype(F32)
    lo = x - hi - mid
    pieces = jnp.where(r == 0, hi, jnp.where(r == 1, mid, jnp.where(r == 2, lo, 0.0)))
    kaug_scr[HEAD_W:HEAD_W + BIAS_ROWS, :] = pieces.astype(BF16)
    kaug_scr[HEAD_W + BIAS_ROWS:, :] = jnp.zeros((HEAD_W - BIAS_ROWS, s_len), BF16)
    vaug_scr[:, :HEAD_W] = v_ref[0]
    vaug_scr[:, HEAD_W:] = jnp.ones((s_len, HEAD_W), BF16)

    lane = lax.broadcasted_iota(jnp.int32, (tq, HEAD_W), 1)
    ones_cols = jnp.where(lane < BIAS_TERMS, 1.0, 0.0).astype(BF16)
    qq_scr[:tq, HEAD_W:] = ones_cols
    qq_scr[tq:, HEAD_W:] = ones_cols
    lam = _lambda(lq1_ref[...], lk1_ref[...], lq2_ref[...], lk2_ref[...], lam_init)

    def load_q(i):
        q = q_ref[0, pl.ds(pl.multiple_of(i * tq, tq), tq), :]
        zero = jnp.zeros_like(q)
        qq_scr[:tq, :HEAD_W] = jnp.where(lane < DK, q, zero)
        qq_scr[tq:, :HEAD_W] = jnp.where(lane >= DK, q, zero)

    def logits(j, s_ref):
        start = pl.multiple_of(j * tk, tk)
        s_ref[...] = _dot(qq_scr[...], kaug_scr[:, pl.ds(start, tk)])

    tri_row = lax.broadcasted_iota(jnp.int32, (tk, tk), 0)
    tri_col = lax.broadcasted_iota(jnp.int32, (tk, tk), 1)

    def causal(blk):
        return jnp.where(tri_col <= tri_row, blk, NEG_INF)

    def absorb(j, s_ref, diag=None):
        start = pl.multiple_of(j * tk, tk)
        s = s_ref[...]
        if diag == 0:
            s = jnp.concatenate([causal(s[:tk]), s[tk:tq], causal(s[tq:tq + tk]), s[tq + tk:]], axis=0)
        elif diag == 1:
            dark = jnp.full((tk, tk), NEG_INF, F32)
            s = jnp.concatenate([dark, causal(s[tk:tq]), dark, causal(s[tq + tk:])], axis=0)
        m_prev = m_scr[...]
        m_new = jnp.maximum(m_prev, jnp.max(s, axis=-1, keepdims=True))
        alpha = jnp.exp2(m_prev - m_new)
        p = jnp.exp2(s - jnp.tile(m_new, (1, tk // HEAD_W)))
        pv = _dot(p.astype(BF16), vaug_scr[pl.ds(start, tk), :])
        l_scr[...] = alpha * l_scr[...] + pv[:, HEAD_W:]
        acc_scr[...] = alpha * acc_scr[...] + pv[:, :HEAD_W]
        m_scr[...] = m_new

    def query_tile(i, carry):
        m_scr[...] = jnp.full(m_scr.shape, NEG_INF, F32)
        l_scr[...] = jnp.zeros(l_scr.shape, F32)
        acc_scr[...] = jnp.zeros(acc_scr.shape, F32)

        def pair(t, c):
            j = 2 * t
            logits(j + 1, sb_scr)
            absorb(j, sa_scr)
            logits(j + 2, sa_scr)
            absorb(j + 1, sb_scr)
            return c

        lax.fori_loop(0, i, pair, 0)
        logits(2 * i + 1, sb_scr)
        absorb(2 * i, sa_scr, diag=0)
        load_q(jnp.minimum(i + 1, n_q - 1))
        logits(0, sa_scr)
        absorb(2 * i + 1, sb_scr, diag=1)

        o = acc_scr[...] / l_scr[...]
        o_ref[0, pl.ds(pl.multiple_of(i * tq, tq), tq), :] = o[:tq] - lam * o[tq:]
        return carry

    load_q(0)
    logits(0, sa_scr)
    lax.fori_loop(0, n_q, query_tile, 0)


def _attn_prompt(q, kt, v, lam_vecs, lam_init, *, tq):
    b, s, d = q.shape
    slopes = jnp.asarray(LOG2E * np.exp2(-np.arange(1, N_HEADS + 1, dtype=np.float64)), F32)
    vec = pl.BlockSpec((1, DK), lambda bb, h: (0, 0))
    rows = pl.BlockSpec((1, s, HEAD_W), lambda bb, h: (bb, 0, h))
    return pl.pallas_call(
        functools.partial(_attn_kernel, tq=tq, lam_init=lam_init),
        grid=(b, N_HEADS),
        in_specs=[
            pl.BlockSpec(memory_space=pltpu.SMEM),
            rows,
            pl.BlockSpec((1, HEAD_W, s), lambda bb, h: (bb, h, 0)),
            rows,
            vec, vec, vec, vec,
        ],
        out_specs=rows,
        out_shape=jax.ShapeDtypeStruct((b, s, d), F32),
        scratch_shapes=[pltpu.VMEM((2 * HEAD_W, s), BF16), pltpu.VMEM((s, 2 * HEAD_W), BF16),
                        pltpu.VMEM((2 * tq, 2 * HEAD_W), BF16)]
                       + [pltpu.VMEM((2 * tq, tq // 2), F32)] * 2
                       + [pltpu.VMEM((2 * tq, HEAD_W), F32)] * 3,
        compiler_params=_cparams("arbitrary", "arbitrary"),
        name="attn_prompt",
    )(slopes, q, kt, v, *lam_vecs)


def _ret_tables(chunk):
    log_g = np.log1p(-np.exp2(-5.0 - np.arange(N_HEADS, dtype=np.float64)))
    idx = np.arange(chunk, dtype=np.float64)
    diff = idx[:, None] - idx[None, :]
    dec = np.where(diff >= 0, np.exp(log_g[:, None, None] * np.maximum(diff, 0.0)), 0.0)
    cross = np.exp(log_g[:, None] * (idx[None, :] + 1.0))
    kdec = np.exp(log_g[:, None] * (chunk - 1.0 - idx[None, :]))
    rep = lambda a: np.repeat(a[:, :, None], HEAD_W, axis=2)
    return (jnp.asarray(dec, F32), jnp.asarray(rep(cross), F32), jnp.asarray(rep(kdec), F32),
            jnp.asarray(np.exp(log_g * chunk), F32))


def _ret_kernel(gl_ref, q_ref, k_ref, v_ref, dec_ref, cross_ref, kdec_ref, o_ref, fin_ref, st_scr):
    pair = pl.program_id(1)
    c = pl.program_id(2)

    @pl.when(c == 0)
    def _():
        st_scr[...] = jnp.zeros(st_scr.shape, F32)

    qb = q_ref[0]
    kb = k_ref[0]
    lane = lax.broadcasted_iota(jnp.int32, qb.shape, 1)
    zero = jnp.zeros_like(qb)
    for hh in range(2):
        msk = (lane < DK) if hh == 0 else (lane >= DK)
        qh = jnp.where(msk, qb, zero)
        kh = jnp.where(msk, kb, zero)
        vh = v_ref[0, :, hh * HEAD_W:(hh + 1) * HEAD_W]
        st = st_scr[hh]
        inner = _dot_nt(qh, kh) * dec_ref[hh]
        o = _dot(inner.astype(BF16), vh) + _dot(qh, st.astype(BF16)) * cross_ref[hh]
        o_ref[0, :, hh * HEAD_W:(hh + 1) * HEAD_W] = o
        kd = (kh.astype(F32) * kdec_ref[hh]).astype(BF16)
        st_scr[hh] = st * gl_ref[2 * pair + hh] + _dot_tn(kd, vh)

    @pl.when(c == pl.num_programs(2) - 1)
    def _():
        fin_ref[0, 0] = st_scr[0, :DK, :]
        fin_ref[0, 1] = st_scr[1, DK:, :]


def _ret_prompt(qr, kr, vr, *, chunk):
    b, s, d = vr.shape
    dec, cross, kdec, gl = _ret_tables(chunk)
    qk = pl.BlockSpec((1, chunk, HEAD_W), lambda bb, p, c: (bb, c, p))
    vo = pl.BlockSpec((1, chunk, 2 * HEAD_W), lambda bb, p, c: (bb, c, p))
    tab = lambda w: pl.BlockSpec((2, chunk, w), lambda bb, p, c: (p, 0, 0))
    return pl.pallas_call(
        _ret_kernel,
        grid=(b, N_HEADS // 2, s // chunk),
        in_specs=[pl.BlockSpec(memory_space=pltpu.SMEM), qk, qk, vo,
                  tab(chunk), tab(HEAD_W), tab(HEAD_W)],
        out_specs=[vo, pl.BlockSpec((1, 2, DK, HEAD_W), lambda bb, p, c: (bb, p, 0, 0))],
        out_shape=[jax.ShapeDtypeStruct((b, s, d), F32),
                   jax.ShapeDtypeStruct((b, N_HEADS, DK, HEAD_W), F32)],
        scratch_shapes=[pltpu.VMEM((2, HEAD_W, HEAD_W), F32)],
        compiler_params=_cparams("arbitrary", "arbitrary", "arbitrary"),
        name="ret_prompt",
    )(gl, qr, kr, vr, dec, cross, kdec)


def _head_norm(t, g):
    parts = []
    for hd in range(N_HEADS):
        tc = t[:, hd * HEAD_W:(hd + 1) * HEAD_W]
        ms = jnp.mean(tc * tc, axis=-1, keepdims=True)
        parts.append(tc * lax.rsqrt(ms + EPS) * g)
    return jnp.concatenate(parts, axis=-1)


def _merge_mlp_kernel(x_ref, oa_ref, or_ref, gr_ref, ga_ref, gb_ref, mod_ref, sg_ref, rg_ref, n2_ref,
                      wo_ref, wu_ref, wd_ref, y_ref, *, per_row_mod, lam_init):
    if per_row_mod:
        gt_a, sh_m, sc_m, gt_m = (mod_ref[c] for c in (2, 3, 4, 5))
    else:
        gt_a, sh_m, sc_m, gt_m = (mod_ref[c, 0] for c in (2, 3, 4, 5))
    ya = _head_norm(oa_ref[...], sg_ref[...]) * (1.0 - lam_init)
    yr = _head_norm(or_ref[...], rg_ref[...]) * _silu(gr_ref[...])
    mixed = jax.nn.sigmoid(ga_ref[...]) * ya + jax.nn.sigmoid(gb_ref[...]) * yr
    x1 = x_ref[...] + gt_a * _dot(mixed.astype(BF16), wo_ref[...])
    h2 = _modulated_norm(x1, n2_ref[...], sh_m, sc_m).astype(BF16)
    u = jnp.maximum(_dot(h2, wu_ref[...]), 0.0)
    y_ref[...] = x1 + gt_m * _dot((u * u).astype(BF16), wd_ref[...])


def _merge_mlp(x, oa, orr, gr, ga, gb, mods, sg, rg, n2, w_out, w_up, w_down, layer, lam_init, *, tm,
               rows_per_batch):
    t, d = x.shape
    per_row = rows_per_batch is None
    if per_row:
        mod_spec = pl.BlockSpec((6, tm, d), lambda i: (0, i, 0))
    else:
        tiles_per_batch = rows_per_batch // tm
        mod_spec = pl.BlockSpec((6, 1, 1, d), lambda i: (0, i // tiles_per_batch, 0, 0))
    row = pl.BlockSpec((tm, d), lambda i: (i, 0))
    const = lambda shp: pl.BlockSpec(shp, lambda i: (0,) * len(shp))
    weight = lambda w: pl.BlockSpec((None,) + w.shape[1:], lambda i: (layer, 0, 0),
                                    pipeline_mode=pl.Buffered(1))
    return pl.pallas_call(
        functools.partial(_merge_mlp_kernel, per_row_mod=per_row, lam_init=lam_init),
        grid=(t // tm,),
        in_specs=[row] * 6 + [mod_spec, const((1, HEAD_W)), const((1, HEAD_W)), const((1, d)),
                               weight(w_out), weight(w_up), weight(w_down)],
        out_specs=row,
        out_shape=jax.ShapeDtypeStruct((t, d), F32),
        compiler_params=_cparams("arbitrary"),
        name="merge_mlp",
    )(x, oa, orr, gr, ga, gb, mods, sg, rg, n2, w_out, w_up, w_down)


def _attn_decode_kernel(pt_ref, slope_ref, expand_ref, q_ref, kn_ref, vn_ref,
                        lq1_ref, lk1_ref, lq2_ref, lk2_ref, *rest, pages_per_step, page, past, lam_init):
    k_refs = rest[:pages_per_step]
    v_refs = rest[pages_per_step:2 * pages_per_step]
    o_ref, kt_scr, v_scr, m_scr, l_scr, acc_scr = rest[2 * pages_per_step:]
    g = pl.program_id(1)
    nrow = 2 * N_HEADS
    d = D_MODEL

    q = q_ref[0].astype(F32)
    row = lax.broadcasted_iota(jnp.int32, (nrow, d), 0)
    lane = lax.broadcasted_iota(jnp.int32, (nrow, d), 1)
    own = (lane // DK) == 2 * (row % N_HEADS) + row // N_HEADS
    qbd = jnp.where(own, jnp.broadcast_to(q, (nrow, d)), 0.0)
    c2 = slope_ref[...]

    @pl.when(g == 0)
    def _():
        kn = kn_ref[0].astype(F32)
        m0 = jnp.sum(qbd * kn, axis=-1, keepdims=True)
        m_scr[...] = jnp.broadcast_to(m0, m_scr.shape)
        l_scr[...] = jnp.ones(l_scr.shape, F32)
        v8 = vn_ref[0].astype(F32)
        acc_scr[...] = jnp.concatenate([v8, v8], axis=0)

    for t in range(pages_per_step):
        kt_scr[:, t * page:(t + 1) * page] = k_refs[t][0, 0].reshape(d, page).astype(BF16)
        v_scr[t // 2, :, (t % 2) * HEAD_W:(t % 2 + 1) * HEAD_W] = (
            v_refs[t][0, 0].reshape(page * N_HEADS, HEAD_W).astype(BF16))
    span = pages_per_step * page
    kpos = g * span + lax.broadcasted_iota(jnp.int32, (1, span), 1)
    s = _dot(qbd.astype(BF16), kt_scr[...]) + (kpos - past).astype(F32) * c2
    m_prev = m_scr[...]
    m_new = jnp.maximum(m_prev, jnp.max(s, axis=-1, keepdims=True))
    alpha = jnp.exp2(m_prev - m_new)
    p = jnp.exp2(s - m_new[:, :1])
    l_new = alpha * l_scr[...] + jnp.sum(p, axis=-1, keepdims=True)
    by_page = jnp.concatenate([p[:, t * page:(t + 1) * page] for t in range(pages_per_step)], axis=0)
    spread = _dot(by_page.astype(BF16), expand_ref[...])
    srow = lax.broadcasted_iota(jnp.int32, spread.shape, 0)
    scol = lax.broadcasted_iota(jnp.int32, spread.shape, 1)
    pe = jnp.where(scol % N_HEADS == srow % N_HEADS, spread, 0.0).astype(BF16)
    acc = alpha * acc_scr[...]
    for u in range(pages_per_step // 2):
        pv = _dot(pe[2 * nrow * u:2 * nrow * (u + 1)], v_scr[u])
        acc = acc + pv[:nrow, :HEAD_W] + pv[nrow:, HEAD_W:]
    m_scr[...] = m_new
    l_scr[...] = l_new
    acc_scr[...] = acc

    @pl.when(g == pl.num_programs(1) - 1)
    def _():
        lam = _lambda(lq1_ref[...], lk1_ref[...], lq2_ref[...], lk2_ref[...], lam_init)
        o = acc / l_new
        o_ref[0] = o[:N_HEADS] - lam * o[N_HEADS:]


def _attn_decode(q, k_new, v_new, cache_kt, cache_v, layer, page_table, lam_vecs, lam_init, *,
                 pages_per_step):
    n, _, d = q.shape
    n_pages = page_table.shape[1]
    page = cache_v.shape[2]
    steps = n_pages // pages_per_step
    slopes = np.tile(LOG2E * np.exp2(-np.arange(1, N_HEADS + 1, dtype=np.float64)), 2)
    slopes = jnp.asarray(np.repeat(slopes[:, None], pages_per_step * page, axis=1), F32)
    expand = jnp.asarray(np.repeat(np.eye(page), N_HEADS, axis=1), BF16)
    tok = pl.BlockSpec((1, 1, d), lambda b, g, pt: (b, 0, 0))
    hv = pl.BlockSpec((1, N_HEADS, HEAD_W), lambda b, g, pt: (b, 0, 0))
    vec = pl.BlockSpec((1, DK), lambda b, g, pt: (0, 0))
    const = lambda a: pl.BlockSpec(a.shape, lambda b, g, pt: (0, 0))

    def phys(b, g, pt, t):
        return pt[b * n_pages + g * pages_per_step + t]

    def k_spec(t):
        return pl.BlockSpec((1, 1, N_HEADS, 2, DK, page),
                            lambda b, g, pt: (layer, phys(b, g, pt, t), 0, 0, 0, 0))

    def v_spec(t):
        return pl.BlockSpec((1, 1, page, N_HEADS, HEAD_W),
                            lambda b, g, pt: (layer, phys(b, g, pt, t), 0, 0, 0))

    kernel = functools.partial(_attn_decode_kernel, pages_per_step=pages_per_step, page=page,
                               past=n_pages * page, lam_init=lam_init)
    return pl.pallas_call(
        kernel,
        grid_spec=pltpu.PrefetchScalarGridSpec(
            num_scalar_prefetch=1,
            grid=(n, steps),
            in_specs=[const(slopes), const(expand), tok, tok, hv, vec, vec, vec, vec]
                     + [k_spec(t) for t in range(pages_per_step)]
                     + [v_spec(t) for t in range(pages_per_step)],
            out_specs=hv,
            scratch_shapes=[pltpu.VMEM((d, pages_per_step * page), BF16),
                            pltpu.VMEM((pages_per_step // 2, page * N_HEADS, 2 * HEAD_W), BF16)]
                           + [pltpu.VMEM((2 * N_HEADS, HEAD_W), F32)] * 3,
        ),
        out_shape=jax.ShapeDtypeStruct((n, N_HEADS, HEAD_W), F32),
        compiler_params=_cparams("arbitrary", "arbitrary"),
        name="attn_decode",
    )(page_table.reshape(-1), slopes, expand, q, k_new, v_new, *lam_vecs,
      *([cache_kt] * pages_per_step), *([cache_v] * pages_per_step))


def _ret_step_kernel(q_ref, k_ref, v_ref, gam_ref, s_ref, o_ref, sn_ref, *, bb):
    rows = N_HEADS * DK
    eye = (lax.broadcasted_iota(jnp.int32, (rows, rows), 0)
           == lax.broadcasted_iota(jnp.int32, (rows, rows), 1))
    gam = gam_ref[...]

    def to_col(r):
        return jnp.sum(jnp.where(eye, jnp.broadcast_to(r, (rows, rows)), 0.0), axis=-1, keepdims=True)

    def head_sum(a):
        return jnp.sum(a.reshape(N_HEADS, DK, a.shape[-1]), axis=1)

    for i in range(bb):
        qcol = to_col(q_ref[i:i + 1, :])
        kcol = to_col(k_ref[i:i + 1, :])
        v8 = v_ref[i]
        vexp = jnp.concatenate(
            [jnp.broadcast_to(v8[hd:hd + 1, :], (DK, HEAD_W)) for hd in range(N_HEADS)], axis=0)
        s0 = s_ref[0, i].reshape(rows, HEAD_W)
        qk = head_sum(jnp.broadcast_to(qcol * kcol, (rows, HEAD_W)))
        o_ref[i] = head_sum(s0 * gam * qcol) + qk * v8
        sn_ref[i] = (s0 * gam + kcol * vexp).reshape(N_HEADS, DK, HEAD_W)


def _ret_step(qr, kr, vr, state, layer, *, bb):
    n = qr.shape[0]
    gamma = 1.0 - np.exp2(-5.0 - np.arange(N_HEADS, dtype=np.float64))
    gam = jnp.asarray(np.repeat(np.repeat(gamma, DK)[:, None], HEAD_W, axis=1), F32)
    qk = pl.BlockSpec((bb, N_HEADS * DK), lambda i: (i, 0))
    hv = pl.BlockSpec((bb, N_HEADS, HEAD_W), lambda i: (i, 0, 0))
    return pl.pallas_call(
        functools.partial(_ret_step_kernel, bb=bb),
        grid=(n // bb,),
        in_specs=[qk, qk, hv, pl.BlockSpec(gam.shape, lambda i: (0, 0)),
                  pl.BlockSpec((1, bb, N_HEADS, DK, HEAD_W), lambda i: (layer, i, 0, 0, 0))],
        out_specs=[hv, pl.BlockSpec((bb, N_HEADS, DK, HEAD_W), lambda i: (i, 0, 0, 0))],
        out_shape=[jax.ShapeDtypeStruct((n, N_HEADS, HEAD_W), F32),
                   jax.ShapeDtypeStruct((n, N_HEADS, DK, HEAD_W), F32)],
        compiler_params=_cparams("arbitrary"),
        name="ret_step",
    )(qr, kr, vr, gam, state)


def kernel(x_prompt, x_sample, cache_k, cache_v, state_ret, page_table, c_prompt, c_sample, ada_w, ada_b, norm1_g, norm2_g, w_in, q_norm_g, k_norm_g, lam_q1, lam_k1, lam_q2, lam_k2, subln_g, ret_norm_g, w_out, w_up, w_down):
    depth = ada_w.shape[0]
    bp, s, d = x_prompt.shape
    ns = x_sample.shape[0]
    ckt = jnp.transpose(cache_k, (0, 1, 3, 4, 5, 2))

    bp_pad = -(-bp // SUBLANES) * SUBLANES
    c_p = jnp.pad(c_prompt, ((0, bp_pad - bp), (0, 0)))
    mods_p, mods_s = _adaln(c_p, c_sample, ada_w, ada_b)
    mods_p = mods_p.reshape(depth, 6, bp_pad, 1, d)
    w_in_b, w_out_b = w_in.astype(BF16), w_out.astype(BF16)
    w_up_b, w_down_b = w_up.astype(BF16), w_down.astype(BF16)

    xp = x_prompt.reshape(bp * s, d)
    xs = x_sample.reshape(ns, d)
    k_all = jnp.zeros((depth, bp, d, s), F32)
    v_all = jnp.zeros((depth, bp * s, d), F32)
    sp_l, ks_l, vs_l, ss_l = [], [], [], []
    for l in range(depth):
        lam_init = 0.8 - 0.6 * math.exp(-0.3 * l)
        lam_vecs = [a[l].reshape(1, DK) for a in (lam_q1, lam_k1, lam_q2, lam_k2)]
        g1, n2 = norm1_g[l].reshape(1, d), norm2_g[l].reshape(1, d)
        qg = jnp.tile(q_norm_g[l], 2).reshape(1, HEAD_W)
        kg = jnp.tile(k_norm_g[l], 2).reshape(1, HEAD_W)
        sg, rg = subln_g[l].reshape(1, HEAD_W), ret_norm_g[l].reshape(1, HEAD_W)

        q, k_all, ktb, _, v_all, vb, qr, kr, vr, gr, ga, gb = _proj(
            xp, mods_p[l], g1, w_in_b, qg, kg, tm=256, rows_per_batch=s,
            stacked=(k_all, v_all), layer=l)
        b3 = lambda a: a.reshape(bp, s, a.shape[-1])
        oa = _attn_prompt(b3(q), ktb, b3(vb), lam_vecs, lam_init, tq=1024)
        orr, s_fin = _ret_prompt(b3(qr), b3(kr), b3(vr), chunk=512)
        xp = _merge_mlp(xp, oa.reshape(bp * s, d), orr.reshape(bp * s, d), gr, ga, gb, mods_p[l],
                        sg, rg, n2, w_out_b, w_up_b, w_down_b, l, lam_init,
                        tm=256, rows_per_batch=s)
        sp_l.append(s_fin)

        q, ktf, _, kb, vf, vb, qr, kr, vr, gr, ga, gb = _proj(
            xs, mods_s[l], g1, w_in_b, qg, kg, tm=ns, layer=l)
        t3 = lambda a: a.reshape(ns, 1, d)
        h3 = lambda a: a.reshape(ns, N_HEADS, HEAD_W)
        oa = _attn_decode(t3(q), t3(kb), h3(vb), ckt, cache_v, l, page_table, lam_vecs, lam_init,
                          pages_per_step=16)
        orr, s_new = _ret_step(qr.astype(F32), kr.astype(F32), h3(vr.astype(F32)), state_ret, l, bb=8)
        xs = _merge_mlp(xs, oa.reshape(ns, d), orr.reshape(ns, d), gr, ga, gb, mods_s[l],
                        sg, rg, n2, w_out_b, w_up_b, w_down_b, l, lam_init,
                        tm=ns, rows_per_batch=None)
        ks_l.append(ktf)
        vs_l.append(vf)
        ss_l.append(s_new)

    h2 = (N_HEADS, 2, DK)
    k_prompt = jnp.transpose(k_all.reshape(depth, bp, *h2, s), (0, 1, 5, 2, 3, 4))
    k_sample = jnp.transpose(jnp.stack(ks_l).reshape(depth, *h2, ns, 1), (0, 4, 5, 1, 2, 3))
    return (xp.reshape(bp, s, d), xs.reshape(ns, 1, d),
            k_prompt,
            v_all.reshape(depth, bp, s, N_HEADS, HEAD_W),
            jnp.stack(sp_l),
            k_sample,
            jnp.stack(vs_l).reshape(depth, ns, 1, N_HEADS, HEAD_W),
            jnp.stack(ss_l))
```

```python
import functools
import math

import numpy as np
import jax
import jax.numpy as jnp
from jax import lax
from jax.experimental import pallas as pl
from jax.experimental.pallas import tpu as pltpu

F32 = jnp.float32
BF16 = jnp.bfloat16

D_MODEL = 1024
N_HEADS = 8
HEAD_W = 128
SUBLANES = 8
DK = 64
N_GROUPS = 8
EPS = 1e-6
NEG_INF = -1e30
LOG2E = 1.4426950408889634
QK_SCALE = DK ** -0.5

VMEM_LIMIT = 56 * 1024 * 1024


def _cparams(*sem):
    return pltpu.CompilerParams(dimension_semantics=sem, vmem_limit_bytes=VMEM_LIMIT)


def _silu(x):
    return x * jax.nn.sigmoid(x)


def _dot(a, b):
    return jnp.dot(a, b, preferred_element_type=F32)


def _dot_nt(a, b):
    return lax.dot_general(a, b, (((1,), (1,)), ((), ())), preferred_element_type=F32)


def _dot_tn(a, b):
    return lax.dot_general(a, b, (((0,), (0,)), ((), ())), preferred_element_type=F32)


def _adaln_kernel(cp_ref, cs_ref, w_ref, b_ref, op_ref, os_ref):
    w = w_ref[0].astype(BF16)
    b = b_ref[0, 0]
    op_ref[0, 0] = _dot(_silu(cp_ref[...]).astype(BF16), w) + b
    os_ref[0, 0] = _dot(_silu(cs_ref[...]).astype(BF16), w) + b


def _adaln(c_p, c_s, ada_w, ada_b):
    depth = ada_w.shape[0]
    bp, bs = c_p.shape[0], c_s.shape[0]
    d = D_MODEL
    b4 = ada_b.reshape(depth, 6, 1, d)
    return pl.pallas_call(
        _adaln_kernel,
        grid=(depth, 6),
        in_specs=[
            pl.BlockSpec((bp, d), lambda l, j: (0, 0)),
            pl.BlockSpec((bs, d), lambda l, j: (0, 0)),
            pl.BlockSpec((1, d, d), lambda l, j: (l, 0, j)),
            pl.BlockSpec((1, 1, 1, d), lambda l, j: (l, j, 0, 0)),
        ],
        out_specs=[
            pl.BlockSpec((1, 1, bp, d), lambda l, j: (l, j, 0, 0)),
            pl.BlockSpec((1, 1, bs, d), lambda l, j: (l, j, 0, 0)),
        ],
        out_shape=[
            jax.ShapeDtypeStruct((depth, 6, bp, d), F32),
            jax.ShapeDtypeStruct((depth, 6, bs, d), F32),
        ],
        compiler_params=_cparams("arbitrary", "arbitrary"),
        name="adaln",
    )(c_p, c_s, ada_w, b4)


def _modulated_norm(x, g, shift, scale):
    ms = jnp.mean(x * x, axis=-1, keepdims=True)
    return (x * lax.rsqrt(ms + EPS) * g) * (1.0 + scale) + shift


def _proj_kernel(x_ref, mod_ref, g1_ref, w_ref, qg_ref, kg_ref, *rest, per_row_mod, n_aliased):
    (q_ref, ktf_ref, ktb_ref, kb_ref, vf_ref, vb_ref, qr_ref, kr_ref, vr_ref,
     gr_ref, ga_ref, gb_ref) = rest[n_aliased:]
    if per_row_mod:
        shift, scale = mod_ref[0], mod_ref[1]
    else:
        shift, scale = mod_ref[0, 0], mod_ref[1, 0]
    h = _modulated_norm(x_ref[...], g1_ref[...], shift, scale).astype(BF16)

    def group(c):
        return _dot(h, w_ref[:, c * D_MODEL:(c + 1) * D_MODEL])

    lo = lax.broadcasted_iota(jnp.int32, (1, HEAD_W), 1) < DK

    def qk_norm(z, g, emit):
        for hd in range(N_HEADS):
            zc = z[:, hd * HEAD_W:(hd + 1) * HEAD_W]
            sq = zc * zc
            s_lo = jnp.sum(jnp.where(lo, sq, 0.0), axis=-1, keepdims=True)
            s_all = jnp.sum(sq, axis=-1, keepdims=True)
            ms = jnp.where(lo, s_lo, s_all - s_lo) * (1.0 / DK)
            emit(hd, zc * lax.rsqrt(ms + EPS) * g)

    def emit_q(hd, y):
        q_ref[:, hd * HEAD_W:(hd + 1) * HEAD_W] = (y * (QK_SCALE * LOG2E)).astype(BF16)

    def emit_k(hd, y):
        yt = y.T
        ktf_ref[hd * HEAD_W:(hd + 1) * HEAD_W, :] = yt
        ktb_ref[hd * HEAD_W:(hd + 1) * HEAD_W, :] = yt.astype(BF16)
        kb_ref[:, hd * HEAD_W:(hd + 1) * HEAD_W] = y.astype(BF16)

    qk_norm(group(0), qg_ref[...], emit_q)
    qk_norm(group(1), kg_ref[...], emit_k)
    va = group(2)
    vf_ref[...] = va
    vb_ref[...] = va.astype(BF16)
    z = group(3)
    half = D_MODEL // 2
    qr_ref[...] = z[:, :half].astype(BF16)
    kr_ref[...] = (z[:, half:] * QK_SCALE).astype(BF16)
    vr_ref[...] = group(4).astype(BF16)
    gr_ref[...] = group(5)
    ga_ref[...] = group(6)
    gb_ref[...] = group(7)


def _proj(x, mods, g1, w_in, qg, kg, *, tm, rows_per_batch=None, stacked=None, layer=None):
    t, d = x.shape
    row = lambda w: pl.BlockSpec((tm, w), lambda i: (i, 0))
    const = lambda shp: pl.BlockSpec(shp, lambda i: (0,) * len(shp))
    sds = jax.ShapeDtypeStruct
    if stacked is None:
        mod_spec = pl.BlockSpec((6, tm, d), lambda i: (0, i, 0))
        kt_spec = pl.BlockSpec((d, tm), lambda i: (0, i))
        ktf = (kt_spec, sds((d, t), F32))
        ktb = (kt_spec, sds((d, t), BF16))
        vf = (row(d), sds((t, d), F32))
        extra_in, extra_specs, aliases = [], [], {}
    else:
        k_all, v_all = stacked
        s = rows_per_batch
        tpb = s // tm
        mod_spec = pl.BlockSpec((6, 1, 1, d), lambda i: (0, i // tpb, 0, 0))
        ktf = (pl.BlockSpec((None, None, d, tm), lambda i: (layer, i // tpb, 0, i % tpb)),
               sds(k_all.shape, F32))
        ktb = (pl.BlockSpec((None, d, tm), lambda i: (i // tpb, 0, i % tpb)), sds((t // s, d, s), BF16))
        vf = (pl.BlockSpec((None, tm, d), lambda i: (layer, i, 0)), sds(v_all.shape, F32))
        extra_in = [k_all, v_all]
        extra_specs = [pl.BlockSpec(memory_space=pl.ANY)] * 2
        aliases = {6: 1, 7: 4}
    plain = lambda w, dt: (row(w), sds((t, w), dt))
    outs = [plain(d, BF16), ktf, ktb, plain(d, BF16), vf, plain(d, BF16),
            plain(d // 2, BF16), plain(d // 2, BF16), plain(d, BF16),
            plain(d, F32), plain(d, F32), plain(d, F32)]
    return pl.pallas_call(
        functools.partial(_proj_kernel, per_row_mod=stacked is None, n_aliased=len(extra_in)),
        grid=(t // tm,),
        in_specs=[
            row(d), mod_spec, const((1, d)),
            pl.BlockSpec((None,) + w_in.shape[1:], lambda i: (layer, 0, 0), pipeline_mode=pl.Buffered(1)),
            const((1, HEAD_W)), const((1, HEAD_W)),
        ] + extra_specs,
        out_specs=[o[0] for o in outs],
        out_shape=[o[1] for o in outs],
        input_output_aliases=aliases,
        compiler_params=_cparams("arbitrary"),
        name="proj",
    )(x, mods, g1, w_in, qg, kg, *extra_in)


def _lambda(lq1, lk1, lq2, lk2, lam_init):
    a = jnp.sum(lq1 * lk1, axis=-1, keepdims=True)
    b = jnp.sum(lq2 * lk2, axis=-1, keepdims=True)
    return jnp.exp(a) - jnp.exp(b) + lam_init


BIAS_ROWS = 16
BIAS_TERMS = 3


def _attn_kernel(slope_ref, q_ref, kt_ref, v_ref, lq1_ref, lk1_ref, lq2_ref, lk2_ref, o_ref,
                 kaug_scr, vaug_scr, qq_scr, sa_scr, sb_scr, m_scr, l_scr, acc_scr, *, tq, lam_init):
    hd = pl.program_id(1)
    tk = tq // 2
    s_len = kt_ref.shape[-1]
    n_q = s_len // tq

    kaug_scr[:HEAD_W, :] = kt_ref[0]
    r = lax.broadcasted_iota(jnp.int32, (BIAS_ROWS, s_len), 0)
    x = lax.broadcasted_iota(jnp.int32, (BIAS_ROWS, s_len), 1).astype(F32) * slope_ref[hd]
    hi = x.astype(BF16).astype(F32)
    mid = (x - hi).astype(BF16).astype(F32)
    lo = x - hi - mid
    pieces = jnp.where(r == 0, hi, jnp.where(r == 1, mid, jnp.where(r == 2, lo, 0.0)))
    kaug_scr[HEAD_W:HEAD_W + BIAS_ROWS, :] = pieces.astype(BF16)
    kaug_scr[HEAD_W + BIAS_ROWS:, :] = jnp.zeros((HEAD_W - BIAS_ROWS, s_len), BF16)
    vaug_scr[:, :HEAD_W] = v_ref[0]
    vaug_scr[:, HEAD_W:] = jnp.ones((s_len, HEAD_W), BF16)

    lane = lax.broadcasted_iota(jnp.int32, (tq, HEAD_W), 1)
    ones_cols = jnp.where(lane < BIAS_TERMS, 1.0, 0.0).astype(BF16)
    qq_scr[:tq, HEAD_W:] = ones_cols
    qq_scr[tq:, HEAD_W:] = ones_cols
    lam = _lambda(lq1_ref[...], lk1_ref[...], lq2_ref[...], lk2_ref[...], lam_init)

    def load_q(i):
        q = q_ref[0, pl.ds(pl.multiple_of(i * tq, tq), tq), :]
        zero = jnp.zeros_like(q)
        qq_scr[:tq, :HEAD_W] = jnp.where(lane < DK, q, zero)
        qq_scr[tq:, :HEAD_W] = jnp.where(lane >= DK, q, zero)

    def logits(j, s_ref):
        start = pl.multiple_of(j * tk, tk)
        s_ref[...] = _dot(qq_scr[...], kaug_scr[:, pl.ds(start, tk)])

    tri_row = lax.broadcasted_iota(jnp.int32, (tk, tk), 0)
    tri_col = lax.broadcasted_iota(jnp.int32, (tk, tk), 1)

    def causal(blk):
        return jnp.where(tri_col <= tri_row, blk, NEG_INF)

    def absorb(j, s_ref, diag=None):
        start = pl.multiple_of(j * tk, tk)
        s = s_ref[...]
        if diag == 0:
            s = jnp.concatenate([causal(s[:tk]), s[tk:tq], causal(s[tq:tq + tk]), s[tq + tk:]], axis=0)
        elif diag == 1:
            dark = jnp.full((tk, tk), NEG_INF, F32)
            s = jnp.concatenate([dark, causal(s[tk:tq]), dark, causal(s[tq + tk:])], axis=0)
        m_prev = m_scr[...]
        m_new = jnp.maximum(m_prev, jnp.max(s, axis=-1, keepdims=True))
        alpha = jnp.exp2(m_prev - m_new)
        p = jnp.exp2(s - jnp.tile(m_new, (1, tk // HEAD_W)))
        pv = _dot(p.astype(BF16), vaug_scr[pl.ds(start, tk), :])
        l_scr[...] = alpha * l_scr[...] + pv[:, HEAD_W:]
        acc_scr[...] = alpha * acc_scr[...] + pv[:, :HEAD_W]
        m_scr[...] = m_new

    def query_tile(i, carry):
        m_scr[...] = jnp.full(m_scr.shape, NEG_INF, F32)
        l_scr[...] = jnp.zeros(l_scr.shape, F32)
        acc_scr[...] = jnp.zeros(acc_scr.shape, F32)

        def pair(t, c):
            j = 2 * t
            logits(j + 1, sb_scr)
            absorb(j, sa_scr)
            logits(j + 2, sa_scr)
            absorb(j + 1, sb_scr)
            return c

        def two_pairs(t, c):
            pair(2 * t, c)
            pair(2 * t + 1, c)
            return c

        lax.fori_loop(0, i // 2, two_pairs, 0)

        @pl.when(i % 2 == 1)
        def _():
            pair(i - 1, 0)

        logits(2 * i + 1, sb_scr)
        absorb(2 * i, sa_scr, diag=0)
        load_q(jnp.minimum(i + 1, n_q - 1))
        logits(0, sa_scr)
        absorb(2 * i + 1, sb_scr, diag=1)

        o = acc_scr[...] / l_scr[...]
        o_ref[0, pl.ds(pl.multiple_of(i * tq, tq), tq), :] = o[:tq] - lam * o[tq:]
        return carry

    load_q(0)
    logits(0, sa_scr)
    lax.fori_loop(0, n_q, query_tile, 0)


def _attn_prompt(q, kt, v, lam_vecs, lam_init, *, tq):
    b, s, d = q.shape
    slopes = jnp.asarray(LOG2E * np.exp2(-np.arange(1, N_HEADS + 1, dtype=np.float64)), F32)
    vec = pl.BlockSpec((1, DK), lambda bb, h: (0, 0))
    rows = pl.BlockSpec((1, s, HEAD_W), lambda bb, h: (bb, 0, h))
    return pl.pallas_call(
        functools.partial(_attn_kernel, tq=tq, lam_init=lam_init),
        grid=(b, N_HEADS),
        in_specs=[
            pl.BlockSpec(memory_space=pltpu.SMEM),
            rows,
            pl.BlockSpec((1, HEAD_W, s), lambda bb, h: (bb, h, 0)),
            rows,
            vec, vec, vec, vec,
        ],
        out_specs=rows,
        out_shape=jax.ShapeDtypeStruct((b, s, d), F32),
        scratch_shapes=[pltpu.VMEM((2 * HEAD_W, s), BF16), pltpu.VMEM((s, 2 * HEAD_W), BF16),
                        pltpu.VMEM((2 * tq, 2 * HEAD_W), BF16)]
                       + [pltpu.VMEM((2 * tq, tq // 2), F32)] * 2
                       + [pltpu.VMEM((2 * tq, HEAD_W), F32)] * 3,
        compiler_params=_cparams("arbitrary", "arbitrary"),
        name="attn_prompt",
    )(slopes, q, kt, v, *lam_vecs)


def _ret_tables(chunk):
    log_g = np.log1p(-np.exp2(-5.0 - np.arange(N_HEADS, dtype=np.float64)))
    idx = np.arange(chunk, dtype=np.float64)
    diff = idx[:, None] - idx[None, :]
    dec = np.where(diff >= 0, np.exp(log_g[:, None, None] * np.maximum(diff, 0.0)), 0.0)
    cross = np.exp(log_g[:, None] * (idx[None, :] + 1.0))
    kdec = np.exp(log_g[:, None] * (chunk - 1.0 - idx[None, :]))
    rep = lambda a: np.repeat(a[:, :, None], HEAD_W, axis=2)
    return (jnp.asarray(dec, F32), jnp.asarray(rep(cross), F32), jnp.asarray(rep(kdec), F32),
            jnp.asarray(np.exp(log_g * chunk), F32))


def _ret_kernel(gl_ref, q_ref, k_ref, v_ref, dec_ref, cross_ref, kdec_ref, o_ref, fin_ref, st_scr):
    pair = pl.program_id(1)
    c = pl.program_id(2)

    @pl.when(c == 0)
    def _():
        st_scr[...] = jnp.zeros(st_scr.shape, F32)

    qb = q_ref[0]
    kb = k_ref[0]
    lane = lax.broadcasted_iota(jnp.int32, qb.shape, 1)
    zero = jnp.zeros_like(qb)
    for hh in range(2):
        msk = (lane < DK) if hh == 0 else (lane >= DK)
        qh = jnp.where(msk, qb, zero)
        kh = jnp.where(msk, kb, zero)
        vh = v_ref[0, :, hh * HEAD_W:(hh + 1) * HEAD_W]
        st = st_scr[hh]
        inner = _dot_nt(qh, kh) * dec_ref[hh]
        o = _dot(inner.astype(BF16), vh) + _dot(qh, st.astype(BF16)) * cross_ref[hh]
        o_ref[0, :, hh * HEAD_W:(hh + 1) * HEAD_W] = o
        kd = (kh.astype(F32) * kdec_ref[hh]).astype(BF16)
        st_scr[hh] = st * gl_ref[2 * pair + hh] + _dot_tn(kd, vh)

    @pl.when(c == pl.num_programs(2) - 1)
    def _():
        fin_ref[0, 0] = st_scr[0, :DK, :]
        fin_ref[0, 1] = st_scr[1, DK:, :]


def _ret_prompt(qr, kr, vr, *, chunk):
    b, s, d = vr.shape
    dec, cross, kdec, gl = _ret_tables(chunk)
    qk = pl.BlockSpec((1, chunk, HEAD_W), lambda bb, p, c: (bb, c, p))
    vo = pl.BlockSpec((1, chunk, 2 * HEAD_W), lambda bb, p, c: (bb, c, p))
    tab = lambda w: pl.BlockSpec((2, chunk, w), lambda bb, p, c: (p, 0, 0))
    return pl.pallas_call(
        _ret_kernel,
        grid=(b, N_HEADS // 2, s // chunk),
        in_specs=[pl.BlockSpec(memory_space=pltpu.SMEM), qk, qk, vo,
                  tab(chunk), tab(HEAD_W), tab(HEAD_W)],
        out_specs=[vo, pl.BlockSpec((1, 2, DK, HEAD_W), lambda bb, p, c: (bb, p, 0, 0))],
        out_shape=[jax.ShapeDtypeStruct((b, s, d), F32),
                   jax.ShapeDtypeStruct((b, N_HEADS, DK, HEAD_W), F32)],
        scratch_shapes=[pltpu.VMEM((2, HEAD_W, HEAD_W), F32)],
        compiler_params=_cparams("arbitrary", "arbitrary", "arbitrary"),
        name="ret_prompt",
    )(gl, qr, kr, vr, dec, cross, kdec)


def _head_norm(t, g):
    parts = []
    for hd in range(N_HEADS):
        tc = t[:, hd * HEAD_W:(hd + 1) * HEAD_W]
        ms = jnp.mean(tc * tc, axis=-1, keepdims=True)
        parts.append(tc * lax.rsqrt(ms + EPS) * g)
    return jnp.concatenate(parts, axis=-1)


def _merge_mlp_kernel(x_ref, oa_ref, or_ref, gr_ref, ga_ref, gb_ref, mod_ref, sg_ref, rg_ref, n2_ref,
                      wo_ref, wu_ref, wd_ref, y_ref, *, per_row_mod, lam_init):
    if per_row_mod:
        gt_a, sh_m, sc_m, gt_m = (mod_ref[c] for c in (2, 3, 4, 5))
    else:
        gt_a, sh_m, sc_m, gt_m = (mod_ref[c, 0] for c in (2, 3, 4, 5))
    ya = _head_norm(oa_ref[...], sg_ref[...]) * (1.0 - lam_init)
    yr = _head_norm(or_ref[...], rg_ref[...]) * _silu(gr_ref[...])
    mixed = jax.nn.sigmoid(ga_ref[...]) * ya + jax.nn.sigmoid(gb_ref[...]) * yr
    x1 = x_ref[...] + gt_a * _dot(mixed.astype(BF16), wo_ref[...])
    h2 = _modulated_norm(x1, n2_ref[...], sh_m, sc_m).astype(BF16)
    u = jnp.maximum(_dot(h2, wu_ref[...]), 0.0)
    y_ref[...] = x1 + gt_m * _dot((u * u).astype(BF16), wd_ref[...])


def _merge_mlp(x, oa, orr, gr, ga, gb, mods, sg, rg, n2, w_out, w_up, w_down, layer, lam_init, *, tm,
               rows_per_batch):
    t, d = x.shape
    per_row = rows_per_batch is None
    if per_row:
        mod_spec = pl.BlockSpec((6, tm, d), lambda i: (0, i, 0))
    else:
        tiles_per_batch = rows_per_batch // tm
        mod_spec = pl.BlockSpec((6, 1, 1, d), lambda i: (0, i // tiles_per_batch, 0, 0))
    row = pl.BlockSpec((tm, d), lambda i: (i, 0))
    const = lambda shp: pl.BlockSpec(shp, lambda i: (0,) * len(shp))
    weight = lambda w: pl.BlockSpec((None,) + w.shape[1:], lambda i: (layer, 0, 0),
                                    pipeline_mode=pl.Buffered(1))
    return pl.pallas_call(
        functools.partial(_merge_mlp_kernel, per_row_mod=per_row, lam_init=lam_init),
        grid=(t // tm,),
        in_specs=[row] * 6 + [mod_spec, const((1, HEAD_W)), const((1, HEAD_W)), const((1, d)),
                               weight(w_out), weight(w_up), weight(w_down)],
        out_specs=row,
        out_shape=jax.ShapeDtypeStruct((t, d), F32),
        compiler_params=_cparams("arbitrary"),
        name="merge_mlp",
    )(x, oa, orr, gr, ga, gb, mods, sg, rg, n2, w_out, w_up, w_down)


def _attn_decode_kernel(pt_ref, slope_ref, expand_ref, q_ref, kn_ref, vn_ref,
                        lq1_ref, lk1_ref, lq2_ref, lk2_ref, *rest, pages_per_step, page, past, lam_init):
    k_refs = rest[:pages_per_step]
    v_refs = rest[pages_per_step:2 * pages_per_step]
    o_ref, kt_scr, v_scr, m_scr, l_scr, acc_scr = rest[2 * pages_per_step:]
    g = pl.program_id(1)
    nrow = 2 * N_HEADS
    d = D_MODEL

    q = q_ref[0].astype(F32)
    row = lax.broadcasted_iota(jnp.int32, (nrow, d), 0)
    lane = lax.broadcasted_iota(jnp.int32, (nrow, d), 1)
    own = (lane // DK) == 2 * (row % N_HEADS) + row // N_HEADS
    qbd = jnp.where(own, jnp.broadcast_to(q, (nrow, d)), 0.0)
    c2 = slope_ref[...]

    @pl.when(g == 0)
    def _():
        kn = kn_ref[0].astype(F32)
        m0 = jnp.sum(qbd * kn, axis=-1, keepdims=True)
        m_scr[...] = jnp.broadcast_to(m0, m_scr.shape)
        l_scr[...] = jnp.ones(l_scr.shape, F32)
        v8 = vn_ref[0].astype(F32)
        acc_scr[...] = jnp.concatenate([v8, v8], axis=0)

    for t in range(pages_per_step):
        kt_scr[:, t * page:(t + 1) * page] = k_refs[t][0, 0].reshape(d, page).astype(BF16)
        v_scr[t // 2, :, (t % 2) * HEAD_W:(t % 2 + 1) * HEAD_W] = (
            v_refs[t][0, 0].reshape(page * N_HEADS, HEAD_W).astype(BF16))
    span = pages_per_step * page
    kpos = g * span + lax.broadcasted_iota(jnp.int32, (1, span), 1)
    s = _dot(qbd.astype(BF16), kt_scr[...]) + (kpos - past).astype(F32) * c2
    m_prev = m_scr[...]
    m_new = jnp.maximum(m_prev, jnp.max(s, axis=-1, keepdims=True))
    alpha = jnp.exp2(m_prev - m_new)
    p = jnp.exp2(s - m_new[:, :1])
    l_new = alpha * l_scr[...] + jnp.sum(p, axis=-1, keepdims=True)
    by_page = jnp.concatenate([p[:, t * page:(t + 1) * page] for t in range(pages_per_step)], axis=0)
    spread = _dot(by_page.astype(BF16), expand_ref[...])
    srow = lax.broadcasted_iota(jnp.int32, spread.shape, 0)
    scol = lax.broadcasted_iota(jnp.int32, spread.shape, 1)
    pe = jnp.where(scol % N_HEADS == srow % N_HEADS, spread, 0.0).astype(BF16)
    acc = alpha * acc_scr[...]
    for u in range(pages_per_step // 2):
        pv = _dot(pe[2 * nrow * u:2 * nrow * (u + 1)], v_scr[u])
        acc = acc + pv[:nrow, :HEAD_W] + pv[nrow:, HEAD_W:]
    m_scr[...] = m_new
    l_scr[...] = l_new
    acc_scr[...] = acc

    @pl.when(g == pl.num_programs(1) - 1)
    def _():
        lam = _lambda(lq1_ref[...], lk1_ref[...], lq2_ref[...], lk2_ref[...], lam_init)
        o = acc / l_new
        o_ref[0] = o[:N_HEADS] - lam * o[N_HEADS:]


def _attn_decode(q, k_new, v_new, cache_kt, cache_v, layer, page_table, lam_vecs, lam_init, *,
                 pages_per_step):
    n, _, d = q.shape
    n_pages = page_table.shape[1]
    page = cache_v.shape[2]
    steps = n_pages // pages_per_step
    slopes = np.tile(LOG2E * np.exp2(-np.arange(1, N_HEADS + 1, dtype=np.float64)), 2)
    slopes = jnp.asarray(np.repeat(slopes[:, None], pages_per_step * page, axis=1), F32)
    expand = jnp.asarray(np.repeat(np.eye(page), N_HEADS, axis=1), BF16)
    tok = pl.BlockSpec((1, 1, d), lambda b, g, pt: (b, 0, 0))
    hv = pl.BlockSpec((1, N_HEADS, HEAD_W), lambda b, g, pt: (b, 0, 0))
    vec = pl.BlockSpec((1, DK), lambda b, g, pt: (0, 0))
    const = lambda a: pl.BlockSpec(a.shape, lambda b, g, pt: (0, 0))

    def phys(b, g, pt, t):
        return pt[b * n_pages + g * pages_per_step + t]

    def k_spec(t):
        return pl.BlockSpec((1, 1, N_HEADS, 2, DK, page),
                            lambda b, g, pt: (layer, phys(b, g, pt, t), 0, 0, 0, 0))

    def v_spec(t):
        return pl.BlockSpec((1, 1, page, N_HEADS, HEAD_W),
                            lambda b, g, pt: (layer, phys(b, g, pt, t), 0, 0, 0))

    kernel = functools.partial(_attn_decode_kernel, pages_per_step=pages_per_step, page=page,
                               past=n_pages * page, lam_init=lam_init)
    return pl.pallas_call(
        kernel,
        grid_spec=pltpu.PrefetchScalarGridSpec(
            num_scalar_prefetch=1,
            grid=(n, steps),
            in_specs=[const(slopes), const(expand), tok, tok, hv, vec, vec, vec, vec]
                     + [k_spec(t) for t in range(pages_per_step)]
                     + [v_spec(t) for t in range(pages_per_step)],
            out_specs=hv,
            scratch_shapes=[pltpu.VMEM((d, pages_per_step * page), BF16),
                            pltpu.VMEM((pages_per_step // 2, page * N_HEADS, 2 * HEAD_W), BF16)]
                           + [pltpu.VMEM((2 * N_HEADS, HEAD_W), F32)] * 3,
        ),
        out_shape=jax.ShapeDtypeStruct((n, N_HEADS, HEAD_W), F32),
        compiler_params=_cparams("arbitrary", "arbitrary"),
        name="attn_decode",
    )(page_table.reshape(-1), slopes, expand, q, k_new, v_new, *lam_vecs,
      *([cache_kt] * pages_per_step), *([cache_v] * pages_per_step))


def _ret_step_kernel(q_ref, k_ref, v_ref, gam_ref, s_ref, o_ref, sn_ref, *, bb):
    rows = N_HEADS * DK
    eye = (lax.broadcasted_iota(jnp.int32, (rows, rows), 0)
           == lax.broadcasted_iota(jnp.int32, (rows, rows), 1))
    gam = gam_ref[...]

    def to_col(r):
        return jnp.sum(jnp.where(eye, jnp.broadcast_to(r, (rows, rows)), 0.0), axis=-1, keepdims=True)

    def head_sum(a):
        return jnp.sum(a.reshape(N_HEADS, DK, a.shape[-1]), axis=1)

    for i in range(bb):
        qcol = to_col(q_ref[i:i + 1, :])
        kcol = to_col(k_ref[i:i + 1, :])
        v8 = v_ref[i]
        vexp = jnp.concatenate(
            [jnp.broadcast_to(v8[hd:hd + 1, :], (DK, HEAD_W)) for hd in range(N_HEADS)], axis=0)
        s0 = s_ref[0, i].reshape(rows, HEAD_W)
        qk = head_sum(jnp.broadcast_to(qcol * kcol, (rows, HEAD_W)))
        o_ref[i] = head_sum(s0 * gam * qcol) + qk * v8
        sn_ref[i] = (s0 * gam + kcol * vexp).reshape(N_HEADS, DK, HEAD_W)


def _ret_step(qr, kr, vr, state, layer, *, bb):
    n = qr.shape[0]
    gamma = 1.0 - np.exp2(-5.0 - np.arange(N_HEADS, dtype=np.float64))
    gam = jnp.asarray(np.repeat(np.repeat(gamma, DK)[:, None], HEAD_W, axis=1), F32)
    qk = pl.BlockSpec((bb, N_HEADS * DK), lambda i: (i, 0))
    hv = pl.BlockSpec((bb, N_HEADS, HEAD_W), lambda i: (i, 0, 0))
    return pl.pallas_call(
        functools.partial(_ret_step_kernel, bb=bb),
        grid=(n // bb,),
        in_specs=[qk, qk, hv, pl.BlockSpec(gam.shape, lambda i: (0, 0)),
                  pl.BlockSpec((1, bb, N_HEADS, DK, HEAD_W), lambda i: (layer, i, 0, 0, 0))],
        out_specs=[hv, pl.BlockSpec((bb, N_HEADS, DK, HEAD_W), lambda i: (i, 0, 0, 0))],
        out_shape=[jax.ShapeDtypeStruct((n, N_HEADS, HEAD_W), F32),
                   jax.ShapeDtypeStruct((n, N_HEADS, DK, HEAD_W), F32)],
        compiler_params=_cparams("arbitrary"),
        name="ret_step",
    )(qr, kr, vr, gam, state)


def kernel(x_prompt, x_sample, cache_k, cache_v, state_ret, page_table, c_prompt, c_sample, ada_w, ada_b, norm1_g, norm2_g, w_in, q_norm_g, k_norm_g, lam_q1, lam_k1, lam_q2, lam_k2, subln_g, ret_norm_g, w_out, w_up, w_down):
    depth = ada_w.shape[0]
    bp, s, d = x_prompt.shape
    ns = x_sample.shape[0]
    ckt = jnp.transpose(cache_k, (0, 1, 3, 4, 5, 2))

    bp_pad = -(-bp // SUBLANES) * SUBLANES
    c_p = jnp.pad(c_prompt, ((0, bp_pad - bp), (0, 0)))
    mods_p, mods_s = _adaln(c_p, c_sample, ada_w, ada_b)
    mods_p = mods_p.reshape(depth, 6, bp_pad, 1, d)
    w_in_b, w_out_b = w_in.astype(BF16), w_out.astype(BF16)
    w_up_b, w_down_b = w_up.astype(BF16), w_down.astype(BF16)

    xp = x_prompt.reshape(bp * s, d)
    xs = x_sample.reshape(ns, d)
    k_all = jnp.zeros((depth, bp, d, s), F32)
    v_all = jnp.zeros((depth, bp * s, d), F32)
    sp_l, ks_l, vs_l, ss_l = [], [], [], []
    for l in range(depth):
        lam_init = 0.8 - 0.6 * math.exp(-0.3 * l)
        lam_vecs = [a[l].reshape(1, DK) for a in (lam_q1, lam_k1, lam_q2, lam_k2)]
        g1, n2 = norm1_g[l].reshape(1, d), norm2_g[l].reshape(1, d)
        qg = jnp.tile(q_norm_g[l], 2).reshape(1, HEAD_W)
        kg = jnp.tile(k_norm_g[l], 2).reshape(1, HEAD_W)
        sg, rg = subln_g[l].reshape(1, HEAD_W), ret_norm_g[l].reshape(1, HEAD_W)

        q, k_all, ktb, _, v_all, vb, qr, kr, vr, gr, ga, gb = _proj(
            xp, mods_p[l], g1, w_in_b, qg, kg, tm=256, rows_per_batch=s,
            stacked=(k_all, v_all), layer=l)
        b3 = lambda a: a.reshape(bp, s, a.shape[-1])
        oa = _attn_prompt(b3(q), ktb, b3(vb), lam_vecs, lam_init, tq=1024)
        orr, s_fin = _ret_prompt(b3(qr), b3(kr), b3(vr), chunk=512)
        xp = _merge_mlp(xp, oa.reshape(bp * s, d), orr.reshape(bp * s, d), gr, ga, gb, mods_p[l],
                        sg, rg, n2, w_out_b, w_up_b, w_down_b, l, lam_init,
                        tm=256, rows_per_batch=s)
        sp_l.append(s_fin)

        q, ktf, _, kb, vf, vb, qr, kr, vr, gr, ga, gb = _proj(
            xs, mods_s[l], g1, w_in_b, qg, kg, tm=ns, layer=l)
        t3 = lambda a: a.reshape(ns, 1, d)
        h3 = lambda a: a.reshape(ns, N_HEADS, HEAD_W)
        oa = _attn_decode(t3(q), t3(kb), h3(vb), ckt, cache_v, l, page_table, lam_vecs, lam_init,
                          pages_per_step=16)
        orr, s_new = _ret_step(qr.astype(F32), kr.astype(F32), h3(vr.astype(F32)), state_ret, l, bb=8)
        xs = _merge_mlp(xs, oa.reshape(ns, d), orr.reshape(ns, d), gr, ga, gb, mods_s[l],
                        sg, rg, n2, w_out_b, w_up_b, w_down_b, l, lam_init,
                        tm=ns, rows_per_batch=None)
        ks_l.append(ktf)
        vs_l.append(vf)
        ss_l.append(s_new)

    h2 = (N_HEADS, 2, DK)
    k_prompt = jnp.transpose(k_all.reshape(depth, bp, *h2, s), (0, 1, 5, 2, 3, 4))
    k_sample = jnp.transpose(jnp.stack(ks_l).reshape(depth, *h2, ns, 1), (0, 4, 5, 1, 2, 3))
    return (xp.reshape(bp, s, d), xs.reshape(ns, 1, d),
            k_prompt,
            v_all.reshape(depth, bp, s, N_HEADS, HEAD_W),
            jnp.stack(sp_l),
            k_sample,
            jnp.stack(vs_l).reshape(depth, ns, 1, N_HEADS, HEAD_W),
            jnp.stack(ss_l))
```

```python
import functools
import math

import numpy as np
import jax
import jax.numpy as jnp
from jax import lax
from jax.experimental import pallas as pl
from jax.experimental.pallas import tpu as pltpu

F32 = jnp.float32
BF16 = jnp.bfloat16

D_MODEL = 1024
N_HEADS = 8
HEAD_W = 128
SUBLANES = 8
DK = 64
N_GROUPS = 8
EPS = 1e-6
NEG_INF = -1e30
LOG2E = 1.4426950408889634
QK_SCALE = DK ** -0.5

VMEM_LIMIT = 56 * 1024 * 1024


def _cparams(*sem):
    return pltpu.CompilerParams(dimension_semantics=sem, vmem_limit_bytes=VMEM_LIMIT)


def _silu(x):
    return x * jax.nn.sigmoid(x)


def _dot(a, b):
    return jnp.dot(a, b, preferred_element_type=F32)


def _dot_nt(a, b):
    return lax.dot_general(a, b, (((1,), (1,)), ((), ())), preferred_element_type=F32)


def _dot_tn(a, b):
    return lax.dot_general(a, b, (((0,), (0,)), ((), ())), preferred_element_type=F32)


def _adaln_kernel(cp_ref, cs_ref, w_ref, b_ref, op_ref, os_ref):
    w = w_ref[0].astype(BF16)
    b = b_ref[0, 0]
    op_ref[0, 0] = _dot(_silu(cp_ref[...]).astype(BF16), w) + b
    os_ref[0, 0] = _dot(_silu(cs_ref[...]).astype(BF16), w) + b


def _adaln(c_p, c_s, ada_w, ada_b):
    depth = ada_w.shape[0]
    bp, bs = c_p.shape[0], c_s.shape[0]
    d = D_MODEL
    b4 = ada_b.reshape(depth, 6, 1, d)
    return pl.pallas_call(
        _adaln_kernel,
        grid=(depth, 6),
        in_specs=[
            pl.BlockSpec((bp, d), lambda l, j: (0, 0)),
            pl.BlockSpec((bs, d), lambda l, j: (0, 0)),
            pl.BlockSpec((1, d, d), lambda l, j: (l, 0, j)),
            pl.BlockSpec((1, 1, 1, d), lambda l, j: (l, j, 0, 0)),
        ],
        out_specs=[
            pl.BlockSpec((1, 1, bp, d), lambda l, j: (l, j, 0, 0)),
            pl.BlockSpec((1, 1, bs, d), lambda l, j: (l, j, 0, 0)),
        ],
        out_shape=[
            jax.ShapeDtypeStruct((depth, 6, bp, d), F32),
            jax.ShapeDtypeStruct((depth, 6, bs, d), F32),
        ],
        compiler_params=_cparams("arbitrary", "arbitrary"),
        name="adaln",
    )(c_p, c_s, ada_w, b4)


def _modulated_norm(x, g, shift, scale):
    ms = jnp.mean(x * x, axis=-1, keepdims=True)
    return (x * lax.rsqrt(ms + EPS) * g) * (1.0 + scale) + shift


def _proj_kernel(x_ref, mod_ref, g1_ref, w_ref, qg_ref, kg_ref, *rest, per_row_mod, n_aliased, new_stack):
    (q_ref, ktf_ref, ktb_ref, kb_ref, vf_ref, vb_ref, qr_ref, kr_ref, vr_ref,
     gr_ref, ga_ref, gb_ref) = rest[n_aliased:]
    if new_stack:
        ktf_ref[1:] = jnp.zeros((ktf_ref.shape[0] - 1,) + ktf_ref.shape[1:], F32)
        vf_ref[1:] = jnp.zeros((vf_ref.shape[0] - 1,) + vf_ref.shape[1:], F32)
        ktf_ref, vf_ref = ktf_ref.at[0], vf_ref.at[0]
    if per_row_mod:
        shift, scale = mod_ref[0], mod_ref[1]
    else:
        shift, scale = mod_ref[0, 0], mod_ref[1, 0]
    h = _modulated_norm(x_ref[...], g1_ref[...], shift, scale).astype(BF16)

    def group(c):
        return _dot(h, w_ref[:, c * D_MODEL:(c + 1) * D_MODEL])

    lo = lax.broadcasted_iota(jnp.int32, (1, HEAD_W), 1) < DK

    def qk_norm(z, g, emit):
        for hd in range(N_HEADS):
            zc = z[:, hd * HEAD_W:(hd + 1) * HEAD_W]
            sq = zc * zc
            s_lo = jnp.sum(jnp.where(lo, sq, 0.0), axis=-1, keepdims=True)
            s_all = jnp.sum(sq, axis=-1, keepdims=True)
            ms = jnp.where(lo, s_lo, s_all - s_lo) * (1.0 / DK)
            emit(hd, zc * lax.rsqrt(ms + EPS) * g)

    def emit_q(hd, y):
        q_ref[:, hd * HEAD_W:(hd + 1) * HEAD_W] = (y * (QK_SCALE * LOG2E)).astype(BF16)

    def emit_k(hd, y):
        yt = y.T
        ktf_ref[hd * HEAD_W:(hd + 1) * HEAD_W, :] = yt
        ktb_ref[hd * HEAD_W:(hd + 1) * HEAD_W, :] = yt.astype(BF16)
        kb_ref[:, hd * HEAD_W:(hd + 1) * HEAD_W] = y.astype(BF16)

    qk_norm(group(0), qg_ref[...], emit_q)
    qk_norm(group(1), kg_ref[...], emit_k)
    va = group(2)
    vf_ref[...] = va
    vb_ref[...] = va.astype(BF16)
    z = group(3)
    half = D_MODEL // 2
    qr_ref[...] = z[:, :half].astype(BF16)
    kr_ref[...] = (z[:, half:] * QK_SCALE).astype(BF16)
    vr_ref[...] = group(4).astype(BF16)
    gr_ref[...] = group(5)
    ga_ref[...] = group(6)
    gb_ref[...] = group(7)


def _proj(x, mods, g1, w_in, qg, kg, *, tm, rows_per_batch=None, stacked=None, layer=None):
    t, d = x.shape
    row = lambda w: pl.BlockSpec((tm, w), lambda i: (i, 0))
    const = lambda shp: pl.BlockSpec(shp, lambda i: (0,) * len(shp))
    sds = jax.ShapeDtypeStruct
    if stacked is None:
        mod_spec = pl.BlockSpec((6, tm, d), lambda i: (0, i, 0))
        kt_spec = pl.BlockSpec((d, tm), lambda i: (0, i))
        ktf = (kt_spec, sds((d, t), F32))
        ktb = (kt_spec, sds((d, t), BF16))
        vf = (row(d), sds((t, d), F32))
        extra_in, extra_specs, aliases = [], [], {}
    else:
        s = rows_per_batch
        tpb = s // tm
        mod_spec = pl.BlockSpec((6, 1, 1, d), lambda i: (0, i // tpb, 0, 0))
        ktb = (pl.BlockSpec((None, d, tm), lambda i: (i // tpb, 0, i % tpb)), sds((t // s, d, s), BF16))
        if isinstance(stacked, int):
            assert layer == 0
            new_layers = stacked
            ktf = (pl.BlockSpec((new_layers, None, d, tm), lambda i: (0, i // tpb, 0, i % tpb)),
                   sds((new_layers, t // s, d, s), F32))
            vf = (pl.BlockSpec((new_layers, tm, d), lambda i: (0, i, 0)), sds((new_layers, t, d), F32))
            extra_in, extra_specs, aliases = [], [], {}
        else:
            k_all, v_all = stacked
            ktf = (pl.BlockSpec((None, None, d, tm), lambda i: (layer, i // tpb, 0, i % tpb)),
                   sds(k_all.shape, F32))
            vf = (pl.BlockSpec((None, tm, d), lambda i: (layer, i, 0)), sds(v_all.shape, F32))
            extra_in = [k_all, v_all]
            extra_specs = [pl.BlockSpec(memory_space=pl.ANY)] * 2
            aliases = {6: 1, 7: 4}
    plain = lambda w, dt: (row(w), sds((t, w), dt))
    outs = [plain(d, BF16), ktf, ktb, plain(d, BF16), vf, plain(d, BF16),
            plain(d // 2, BF16), plain(d // 2, BF16), plain(d, BF16),
            plain(d, F32), plain(d, F32), plain(d, F32)]
    return pl.pallas_call(
        functools.partial(_proj_kernel, per_row_mod=stacked is None, n_aliased=len(extra_in),
                          new_stack=isinstance(stacked, int)),
        grid=(t // tm,),
        in_specs=[
            row(d), mod_spec, const((1, d)),
            pl.BlockSpec((None,) + w_in.shape[1:], lambda i: (layer, 0, 0), pipeline_mode=pl.Buffered(1)),
            const((1, HEAD_W)), const((1, HEAD_W)),
        ] + extra_specs,
        out_specs=[o[0] for o in outs],
        out_shape=[o[1] for o in outs],
        input_output_aliases=aliases,
        compiler_params=_cparams("arbitrary"),
        name="proj",
    )(x, mods, g1, w_in, qg, kg, *extra_in)


def _lambda(lq1, lk1, lq2, lk2, lam_init):
    a = jnp.sum(lq1 * lk1, axis=-1, keepdims=True)
    b = jnp.sum(lq2 * lk2, axis=-1, keepdims=True)
    return jnp.exp(a) - jnp.exp(b) + lam_init


BIAS_ROWS = 16
BIAS_TERMS = 3


def _attn_kernel(slope_ref, q_ref, kt_ref, v_ref, lq1_ref, lk1_ref, lq2_ref, lk2_ref, o_ref,
                 kaug_scr, vaug_scr, qq_scr, sa_scr, sb_scr, m_scr, l_scr, acc_scr, *, tq, lam_init):
    hd = pl.program_id(1)
    tk = tq // 2
    s_len = kt_ref.shape[-1]
    n_q = s_len // tq

    kaug_scr[:HEAD_W, :] = kt_ref[0]
    r = lax.broadcasted_iota(jnp.int32, (BIAS_ROWS, s_len), 0)
    x = lax.broadcasted_iota(jnp.int32, (BIAS_ROWS, s_len), 1).astype(F32) * slope_ref[hd]
    hi = x.astype(BF16).astype(F32)
    mid = (x - hi).astype(BF16).astype(F32)
    lo = x - hi - mid
    pieces = jnp.where(r == 0, hi, jnp.where(r == 1, mid, jnp.where(r == 2, lo, 0.0)))
    kaug_scr[HEAD_W:HEAD_W + BIAS_ROWS, :] = pieces.astype(BF16)
    kaug_scr[HEAD_W + BIAS_ROWS:, :] = jnp.zeros((HEAD_W - BIAS_ROWS, s_len), BF16)
    vaug_scr[:, :HEAD_W] = v_ref[0]
    vaug_scr[:, HEAD_W:] = jnp.ones((s_len, HEAD_W), BF16)

    lane = lax.broadcasted_iota(jnp.int32, (tq, HEAD_W), 1)
    ones_cols = jnp.where(lane < BIAS_TERMS, 1.0, 0.0).astype(BF16)
    qq_scr[:tq, HEAD_W:] = ones_cols
    qq_scr[tq:, HEAD_W:] = ones_cols
    lam = _lambda(lq1_ref[...], lk1_ref[...], lq2_ref[...], lk2_ref[...], lam_init)

    def load_q(i):
        q = q_ref[0, pl.ds(pl.multiple_of(i * tq, tq), tq), :]
        zero = jnp.zeros_like(q)
        qq_scr[:tq, :HEAD_W] = jnp.where(lane < DK, q, zero)
        qq_scr[tq:, :HEAD_W] = jnp.where(lane >= DK, q, zero)

    def logits(j, s_ref):
        start = pl.multiple_of(j * tk, tk)
        s_ref[...] = _dot(qq_scr[...], kaug_scr[:, pl.ds(start, tk)])

    tri_row = lax.broadcasted_iota(jnp.int32, (tk, tk), 0)
    tri_col = lax.broadcasted_iota(jnp.int32, (tk, tk), 1)

    def causal(blk):
        return jnp.where(tri_col <= tri_row, blk, NEG_INF)

    def absorb(j, s_ref, diag=None):
        start = pl.multiple_of(j * tk, tk)
        s = s_ref[...]
        if diag == 0:
            s = jnp.concatenate([causal(s[:tk]), s[tk:tq], causal(s[tq:tq + tk]), s[tq + tk:]], axis=0)
        elif diag == 1:
            dark = jnp.full((tk, tk), NEG_INF, F32)
            s = jnp.concatenate([dark, causal(s[tk:tq]), dark, causal(s[tq + tk:])], axis=0)
        m_prev = m_scr[...]
        m_new = jnp.maximum(m_prev, jnp.max(s, axis=-1, keepdims=True))
        alpha = jnp.exp2(m_prev - m_new)
        p = jnp.exp2(s - jnp.tile(m_new, (1, tk // HEAD_W)))
        pv = _dot(p.astype(BF16), vaug_scr[pl.ds(start, tk), :])
        l_scr[...] = alpha * l_scr[...] + pv[:, HEAD_W:]
        acc_scr[...] = alpha * acc_scr[...] + pv[:, :HEAD_W]
        m_scr[...] = m_new

    def query_tile(i, carry):
        m_scr[...] = jnp.full(m_scr.shape, NEG_INF, F32)
        l_scr[...] = jnp.zeros(l_scr.shape, F32)
        acc_scr[...] = jnp.zeros(acc_scr.shape, F32)

        def pair(t, c):
            j = 2 * t
            logits(j + 1, sb_scr)
            absorb(j, sa_scr)
            logits(j + 2, sa_scr)
            absorb(j + 1, sb_scr)
            return c

        def two_pairs(t, c):
            pair(2 * t, c)
            pair(2 * t + 1, c)
            return c

        lax.fori_loop(0, i // 2, two_pairs, 0)

        @pl.when(i % 2 == 1)
        def _():
            pair(i - 1, 0)

        logits(2 * i + 1, sb_scr)
        absorb(2 * i, sa_scr, diag=0)
        load_q(jnp.minimum(i + 1, n_q - 1))
        logits(0, sa_scr)
        absorb(2 * i + 1, sb_scr, diag=1)

        o = acc_scr[...] / l_scr[...]
        o_ref[0, pl.ds(pl.multiple_of(i * tq, tq), tq), :] = o[:tq] - lam * o[tq:]
        return carry

    load_q(0)
    logits(0, sa_scr)
    lax.fori_loop(0, n_q, query_tile, 0)


def _attn_prompt(q, kt, v, lam_vecs, lam_init, *, tq):
    b, s, d = q.shape
    slopes = jnp.asarray(LOG2E * np.exp2(-np.arange(1, N_HEADS + 1, dtype=np.float64)), F32)
    vec = pl.BlockSpec((1, DK), lambda bb, h: (0, 0))
    rows = pl.BlockSpec((1, s, HEAD_W), lambda bb, h: (bb, 0, h))
    return pl.pallas_call(
        functools.partial(_attn_kernel, tq=tq, lam_init=lam_init),
        grid=(b, N_HEADS),
        in_specs=[
            pl.BlockSpec(memory_space=pltpu.SMEM),
            rows,
            pl.BlockSpec((1, HEAD_W, s), lambda bb, h: (bb, h, 0)),
            rows,
            vec, vec, vec, vec,
        ],
        out_specs=rows,
        out_shape=jax.ShapeDtypeStruct((b, s, d), F32),
        scratch_shapes=[pltpu.VMEM((2 * HEAD_W, s), BF16), pltpu.VMEM((s, 2 * HEAD_W), BF16),
                        pltpu.VMEM((2 * tq, 2 * HEAD_W), BF16)]
                       + [pltpu.VMEM((2 * tq, tq // 2), F32)] * 2
                       + [pltpu.VMEM((2 * tq, HEAD_W), F32)] * 3,
        compiler_params=_cparams("arbitrary", "arbitrary"),
        name="attn_prompt",
    )(slopes, q, kt, v, *lam_vecs)


def _ret_tables(chunk):
    log_g = np.log1p(-np.exp2(-5.0 - np.arange(N_HEADS, dtype=np.float64)))
    idx = np.arange(chunk, dtype=np.float64)
    diff = idx[:, None] - idx[None, :]
    dec = np.where(diff >= 0, np.exp(log_g[:, None, None] * np.maximum(diff, 0.0)), 0.0)
    cross = np.exp(log_g[:, None] * (idx[None, :] + 1.0))
    kdec = np.exp(log_g[:, None] * (chunk - 1.0 - idx[None, :]))
    rep = lambda a: np.repeat(a[:, :, None], HEAD_W, axis=2)
    return (jnp.asarray(dec, F32), jnp.asarray(rep(cross), F32), jnp.asarray(rep(kdec), F32),
            jnp.asarray(np.exp(log_g * chunk), F32))


def _ret_kernel(gl_ref, q_ref, k_ref, v_ref, dec_ref, cross_ref, kdec_ref, o_ref, fin_ref, st_scr):
    pair = pl.program_id(1)
    c = pl.program_id(2)

    @pl.when(c == 0)
    def _():
        st_scr[...] = jnp.zeros(st_scr.shape, F32)

    qb = q_ref[0]
    kb = k_ref[0]
    lane = lax.broadcasted_iota(jnp.int32, qb.shape, 1)
    zero = jnp.zeros_like(qb)
    for hh in range(2):
        msk = (lane < DK) if hh == 0 else (lane >= DK)
        qh = jnp.where(msk, qb, zero)
        kh = jnp.where(msk, kb, zero)
        vh = v_ref[0, :, hh * HEAD_W:(hh + 1) * HEAD_W]
        st = st_scr[hh]
        inner = _dot_nt(qh, kh) * dec_ref[hh]
        o = _dot(inner.astype(BF16), vh) + _dot(qh, st.astype(BF16)) * cross_ref[hh]
        o_ref[0, :, hh * HEAD_W:(hh + 1) * HEAD_W] = o
        kd = (kh.astype(F32) * kdec_ref[hh]).astype(BF16)
        st_scr[hh] = st * gl_ref[2 * pair + hh] + _dot_tn(kd, vh)

    @pl.when(c == pl.num_programs(2) - 1)
    def _():
        fin_ref[0, 0] = st_scr[0, :DK, :]
        fin_ref[0, 1] = st_scr[1, DK:, :]


def _ret_prompt(qr, kr, vr, *, chunk):
    b, s, d = vr.shape
    dec, cross, kdec, gl = _ret_tables(chunk)
    qk = pl.BlockSpec((1, chunk, HEAD_W), lambda bb, p, c: (bb, c, p))
    vo = pl.BlockSpec((1, chunk, 2 * HEAD_W), lambda bb, p, c: (bb, c, p))
    tab = lambda w: pl.BlockSpec((2, chunk, w), lambda bb, p, c: (p, 0, 0))
    return pl.pallas_call(
        _ret_kernel,
        grid=(b, N_HEADS // 2, s // chunk),
        in_specs=[pl.BlockSpec(memory_space=pltpu.SMEM), qk, qk, vo,
                  tab(chunk), tab(HEAD_W), tab(HEAD_W)],
        out_specs=[vo, pl.BlockSpec((1, 2, DK, HEAD_W), lambda bb, p, c: (bb, p, 0, 0))],
        out_shape=[jax.ShapeDtypeStruct((b, s, d), F32),
                   jax.ShapeDtypeStruct((b, N_HEADS, DK, HEAD_W), F32)],
        scratch_shapes=[pltpu.VMEM((2, HEAD_W, HEAD_W), F32)],
        compiler_params=_cparams("arbitrary", "arbitrary", "arbitrary"),
        name="ret_prompt",
    )(gl, qr, kr, vr, dec, cross, kdec)


def _head_norm(t, g):
    parts = []
    for hd in range(N_HEADS):
        tc = t[:, hd * HEAD_W:(hd + 1) * HEAD_W]
        ms = jnp.mean(tc * tc, axis=-1, keepdims=True)
        parts.append(tc * lax.rsqrt(ms + EPS) * g)
    return jnp.concatenate(parts, axis=-1)


def _merge_mlp_kernel(x_ref, oa_ref, or_ref, gr_ref, ga_ref, gb_ref, mod_ref, sg_ref, rg_ref, n2_ref,
                      wo_ref, wu_ref, wd_ref, y_ref, *, per_row_mod, lam_init):
    if per_row_mod:
        gt_a, sh_m, sc_m, gt_m = (mod_ref[c] for c in (2, 3, 4, 5))
    else:
        gt_a, sh_m, sc_m, gt_m = (mod_ref[c, 0] for c in (2, 3, 4, 5))
    ya = _head_norm(oa_ref[...], sg_ref[...]) * (1.0 - lam_init)
    yr = _head_norm(or_ref[...], rg_ref[...]) * _silu(gr_ref[...])
    mixed = jax.nn.sigmoid(ga_ref[...]) * ya + jax.nn.sigmoid(gb_ref[...]) * yr
    x1 = x_ref[...] + gt_a * _dot(mixed.astype(BF16), wo_ref[...])
    h2 = _modulated_norm(x1, n2_ref[...], sh_m, sc_m).astype(BF16)
    u = jnp.maximum(_dot(h2, wu_ref[...]), 0.0)
    y_ref[...] = x1 + gt_m * _dot((u * u).astype(BF16), wd_ref[...])


def _merge_mlp(x, oa, orr, gr, ga, gb, mods, sg, rg, n2, w_out, w_up, w_down, layer, lam_init, *, tm,
               rows_per_batch):
    t, d = x.shape
    per_row = rows_per_batch is None
    if per_row:
        mod_spec = pl.BlockSpec((6, tm, d), lambda i: (0, i, 0))
    else:
        tiles_per_batch = rows_per_batch // tm
        mod_spec = pl.BlockSpec((6, 1, 1, d), lambda i: (0, i // tiles_per_batch, 0, 0))
    row = pl.BlockSpec((tm, d), lambda i: (i, 0))
    const = lambda shp: pl.BlockSpec(shp, lambda i: (0,) * len(shp))
    weight = lambda w: pl.BlockSpec((None,) + w.shape[1:], lambda i: (layer, 0, 0),
                                    pipeline_mode=pl.Buffered(1))
    return pl.pallas_call(
        functools.partial(_merge_mlp_kernel, per_row_mod=per_row, lam_init=lam_init),
        grid=(t // tm,),
        in_specs=[row] * 6 + [mod_spec, const((1, HEAD_W)), const((1, HEAD_W)), const((1, d)),
                               weight(w_out), weight(w_up), weight(w_down)],
        out_specs=row,
        out_shape=jax.ShapeDtypeStruct((t, d), F32),
        compiler_params=_cparams("arbitrary"),
        name="merge_mlp",
    )(x, oa, orr, gr, ga, gb, mods, sg, rg, n2, w_out, w_up, w_down)


def _attn_decode_kernel(pt_ref, slope_ref, expand_ref, q_ref, kn_ref, vn_ref,
                        lq1_ref, lk1_ref, lq2_ref, lk2_ref, *rest, pages_per_step, page, past, lam_init):
    k_refs = rest[:pages_per_step]
    v_refs = rest[pages_per_step:2 * pages_per_step]
    o_ref, kt_scr, v_scr, m_scr, l_scr, acc_scr = rest[2 * pages_per_step:]
    g = pl.program_id(1)
    nrow = 2 * N_HEADS
    d = D_MODEL

    q = q_ref[0].astype(F32)
    row = lax.broadcasted_iota(jnp.int32, (nrow, d), 0)
    lane = lax.broadcasted_iota(jnp.int32, (nrow, d), 1)
    own = (lane // DK) == 2 * (row % N_HEADS) + row // N_HEADS
    qbd = jnp.where(own, jnp.broadcast_to(q, (nrow, d)), 0.0)
    c2 = slope_ref[...]

    @pl.when(g == 0)
    def _():
        kn = kn_ref[0].astype(F32)
        m0 = jnp.sum(qbd * kn, axis=-1, keepdims=True)
        m_scr[...] = jnp.broadcast_to(m0, m_scr.shape)
        l_scr[...] = jnp.ones(l_scr.shape, F32)
        v8 = vn_ref[0].astype(F32)
        acc_scr[...] = jnp.concatenate([v8, v8], axis=0)

    for t in range(pages_per_step):
        kt_scr[:, t * page:(t + 1) * page] = k_refs[t][0, 0].reshape(d, page).astype(BF16)
        v_scr[t // 2, :, (t % 2) * HEAD_W:(t % 2 + 1) * HEAD_W] = (
            v_refs[t][0, 0].reshape(page * N_HEADS, HEAD_W).astype(BF16))
    span = pages_per_step * page
    kpos = g * span + lax.broadcasted_iota(jnp.int32, (1, span), 1)
    s = _dot(qbd.astype(BF16), kt_scr[...]) + (kpos - past).astype(F32) * c2
    m_prev = m_scr[...]
    m_new = jnp.maximum(m_prev, jnp.max(s, axis=-1, keepdims=True))
    alpha = jnp.exp2(m_prev - m_new)
    p = jnp.exp2(s - m_new[:, :1])
    l_new = alpha * l_scr[...] + jnp.sum(p, axis=-1, keepdims=True)
    by_page = jnp.concatenate([p[:, t * page:(t + 1) * page] for t in range(pages_per_step)], axis=0)
    spread = _dot(by_page.astype(BF16), expand_ref[...])
    srow = lax.broadcasted_iota(jnp.int32, spread.shape, 0)
    scol = lax.broadcasted_iota(jnp.int32, spread.shape, 1)
    pe = jnp.where(scol % N_HEADS == srow % N_HEADS, spread, 0.0).astype(BF16)
    acc = alpha * acc_scr[...]
    for u in range(pages_per_step // 2):
        pv = _dot(pe[2 * nrow * u:2 * nrow * (u + 1)], v_scr[u])
        acc = acc + pv[:nrow, :HEAD_W] + pv[nrow:, HEAD_W:]
    m_scr[...] = m_new
    l_scr[...] = l_new
    acc_scr[...] = acc

    @pl.when(g == pl.num_programs(1) - 1)
    def _():
        lam = _lambda(lq1_ref[...], lk1_ref[...], lq2_ref[...], lk2_ref[...], lam_init)
        o = acc / l_new
        o_ref[0] = o[:N_HEADS] - lam * o[N_HEADS:]


def _attn_decode(q, k_new, v_new, cache_kt, cache_v, layer, page_table, lam_vecs, lam_init, *,
                 pages_per_step):
    n, _, d = q.shape
    n_pages = page_table.shape[1]
    page = cache_v.shape[2]
    steps = n_pages // pages_per_step
    slopes = np.tile(LOG2E * np.exp2(-np.arange(1, N_HEADS + 1, dtype=np.float64)), 2)
    slopes = jnp.asarray(np.repeat(slopes[:, None], pages_per_step * page, axis=1), F32)
    expand = jnp.asarray(np.repeat(np.eye(page), N_HEADS, axis=1), BF16)
    tok = pl.BlockSpec((1, 1, d), lambda b, g, pt: (b, 0, 0))
    hv = pl.BlockSpec((1, N_HEADS, HEAD_W), lambda b, g, pt: (b, 0, 0))
    vec = pl.BlockSpec((1, DK), lambda b, g, pt: (0, 0))
    const = lambda a: pl.BlockSpec(a.shape, lambda b, g, pt: (0, 0))

    def phys(b, g, pt, t):
        return pt[b * n_pages + g * pages_per_step + t]

    def k_spec(t):
        return pl.BlockSpec((1, 1, N_HEADS, 2, DK, page),
                            lambda b, g, pt: (layer, phys(b, g, pt, t), 0, 0, 0, 0))

    def v_spec(t):
        return pl.BlockSpec((1, 1, page, N_HEADS, HEAD_W),
                            lambda b, g, pt: (layer, phys(b, g, pt, t), 0, 0, 0))

    kernel = functools.partial(_attn_decode_kernel, pages_per_step=pages_per_step, page=page,
                               past=n_pages * page, lam_init=lam_init)
    return pl.pallas_call(
        kernel,
        grid_spec=pltpu.PrefetchScalarGridSpec(
            num_scalar_prefetch=1,
            grid=(n, steps),
            in_specs=[const(slopes), const(expand), tok, tok, hv, vec, vec, vec, vec]
                     + [k_spec(t) for t in range(pages_per_step)]
                     + [v_spec(t) for t in range(pages_per_step)],
            out_specs=hv,
            scratch_shapes=[pltpu.VMEM((d, pages_per_step * page), BF16),
                            pltpu.VMEM((pages_per_step // 2, page * N_HEADS, 2 * HEAD_W), BF16)]
                           + [pltpu.VMEM((2 * N_HEADS, HEAD_W), F32)] * 3,
        ),
        out_shape=jax.ShapeDtypeStruct((n, N_HEADS, HEAD_W), F32),
        compiler_params=_cparams("arbitrary", "arbitrary"),
        name="attn_decode",
    )(page_table.reshape(-1), slopes, expand, q, k_new, v_new, *lam_vecs,
      *([cache_kt] * pages_per_step), *([cache_v] * pages_per_step))


def _ret_step_kernel(q_ref, k_ref, v_ref, gam_ref, s_ref, o_ref, sn_ref, *, bb):
    rows = N_HEADS * DK
    eye = (lax.broadcasted_iota(jnp.int32, (rows, rows), 0)
           == lax.broadcasted_iota(jnp.int32, (rows, rows), 1))
    gam = gam_ref[...]

    def to_col(r):
        return jnp.sum(jnp.where(eye, jnp.broadcast_to(r, (rows, rows)), 0.0), axis=-1, keepdims=True)

    def head_sum(a):
        return jnp.sum(a.reshape(N_HEADS, DK, a.shape[-1]), axis=1)

    for i in range(bb):
        qcol = to_col(q_ref[i:i + 1, :])
        kcol = to_col(k_ref[i:i + 1, :])
        v8 = v_ref[i]
        vexp = jnp.concatenate(
            [jnp.broadcast_to(v8[hd:hd + 1, :], (DK, HEAD_W)) for hd in range(N_HEADS)], axis=0)
        s0 = s_ref[0, i].reshape(rows, HEAD_W)
        qk = head_sum(jnp.broadcast_to(qcol * kcol, (rows, HEAD_W)))
        o_ref[i] = head_sum(s0 * gam * qcol) + qk * v8
        sn_ref[i] = (s0 * gam + kcol * vexp).reshape(N_HEADS, DK, HEAD_W)


def _ret_step(qr, kr, vr, state, layer, *, bb):
    n = qr.shape[0]
    gamma = 1.0 - np.exp2(-5.0 - np.arange(N_HEADS, dtype=np.float64))
    gam = jnp.asarray(np.repeat(np.repeat(gamma, DK)[:, None], HEAD_W, axis=1), F32)
    qk = pl.BlockSpec((bb, N_HEADS * DK), lambda i: (i, 0))
    hv = pl.BlockSpec((bb, N_HEADS, HEAD_W), lambda i: (i, 0, 0))
    return pl.pallas_call(
        functools.partial(_ret_step_kernel, bb=bb),
        grid=(n // bb,),
        in_specs=[qk, qk, hv, pl.BlockSpec(gam.shape, lambda i: (0, 0)),
                  pl.BlockSpec((1, bb, N_HEADS, DK, HEAD_W), lambda i: (layer, i, 0, 0, 0))],
        out_specs=[hv, pl.BlockSpec((bb, N_HEADS, DK, HEAD_W), lambda i: (i, 0, 0, 0))],
        out_shape=[jax.ShapeDtypeStruct((n, N_HEADS, HEAD_W), F32),
                   jax.ShapeDtypeStruct((n, N_HEADS, DK, HEAD_W), F32)],
        compiler_params=_cparams("arbitrary"),
        name="ret_step",
    )(qr, kr, vr, gam, state)


def kernel(x_prompt, x_sample, cache_k, cache_v, state_ret, page_table, c_prompt, c_sample, ada_w, ada_b, norm1_g, norm2_g, w_in, q_norm_g, k_norm_g, lam_q1, lam_k1, lam_q2, lam_k2, subln_g, ret_norm_g, w_out, w_up, w_down):
    depth = ada_w.shape[0]
    bp, s, d = x_prompt.shape
    ns = x_sample.shape[0]
    ckt = jnp.transpose(cache_k, (0, 1, 3, 4, 5, 2))

    bp_pad = -(-bp // SUBLANES) * SUBLANES
    c_p = jnp.pad(c_prompt, ((0, bp_pad - bp), (0, 0)))
    mods_p, mods_s = _adaln(c_p, c_sample, ada_w, ada_b)
    mods_p = mods_p.reshape(depth, 6, bp_pad, 1, d)
    w_in_b, w_out_b = w_in.astype(BF16), w_out.astype(BF16)
    w_up_b, w_down_b = w_up.astype(BF16), w_down.astype(BF16)

    xp = x_prompt.reshape(bp * s, d)
    xs = x_sample.reshape(ns, d)
    k_all = v_all = None
    sp_l, ks_l, vs_l, ss_l = [], [], [], []
    for l in range(depth):
        lam_init = 0.8 - 0.6 * math.exp(-0.3 * l)
        lam_vecs = [a[l].reshape(1, DK) for a in (lam_q1, lam_k1, lam_q2, lam_k2)]
        g1, n2 = norm1_g[l].reshape(1, d), norm2_g[l].reshape(1, d)
        qg = jnp.tile(q_norm_g[l], 2).reshape(1, HEAD_W)
        kg = jnp.tile(k_norm_g[l], 2).reshape(1, HEAD_W)
        sg, rg = subln_g[l].reshape(1, HEAD_W), ret_norm_g[l].reshape(1, HEAD_W)

        q, k_all, ktb, _, v_all, vb, qr, kr, vr, gr, ga, gb = _proj(
            xp, mods_p[l], g1, w_in_b, qg, kg, tm=256, rows_per_batch=s,
            stacked=depth if l == 0 else (k_all, v_all), layer=l)
        b3 = lambda a: a.reshape(bp, s, a.shape[-1])
        oa = _attn_prompt(b3(q), ktb, b3(vb), lam_vecs, lam_init, tq=1024)
        orr, s_fin = _ret_prompt(b3(qr), b3(kr), b3(vr), chunk=512)
        xp = _merge_mlp(xp, oa.reshape(bp * s, d), orr.reshape(bp * s, d), gr, ga, gb, mods_p[l],
                        sg, rg, n2, w_out_b, w_up_b, w_down_b, l, lam_init,
                        tm=256, rows_per_batch=s)
        sp_l.append(s_fin)

        q, ktf, _, kb, vf, vb, qr, kr, vr, gr, ga, gb = _proj(
            xs, mods_s[l], g1, w_in_b, qg, kg, tm=ns, layer=l)
        t3 = lambda a: a.reshape(ns, 1, d)
        h3 = lambda a: a.reshape(ns, N_HEADS, HEAD_W)
        oa = _attn_decode(t3(q), t3(kb), h3(vb), ckt, cache_v, l, page_table, lam_vecs, lam_init,
                          pages_per_step=16)
        orr, s_new = _ret_step(qr.astype(F32), kr.astype(F32), h3(vr.astype(F32)), state_ret, l, bb=8)
        xs = _merge_mlp(xs, oa.reshape(ns, d), orr.reshape(ns, d), gr, ga, gb, mods_s[l],
                        sg, rg, n2, w_out_b, w_up_b, w_down_b, l, lam_init,
                        tm=ns, rows_per_batch=None)
        ks_l.append(ktf)
        vs_l.append(vf)
        ss_l.append(s_new)

    h2 = (N_HEADS, 2, DK)
    k_prompt = jnp.transpose(k_all.reshape(depth, bp, *h2, s), (0, 1, 5, 2, 3, 4))
    k_sample = jnp.transpose(jnp.stack(ks_l).reshape(depth, *h2, ns, 1), (0, 4, 5, 1, 2, 3))
    return (xp.reshape(bp, s, d), xs.reshape(ns, 1, d),
            k_prompt,
            v_all.reshape(depth, bp, s, N_HEADS, HEAD_W),
            jnp.stack(sp_l),
            k_sample,
            jnp.stack(vs_l).reshape(depth, ns, 1, N_HEADS, HEAD_W),
            jnp.stack(ss_l))
```
